```python
import math
import jax, jax.numpy as jnp
from jax import lax
import numpy as np

D_MODEL = 1024
BATCH = 1
SEQ = 16384
DEPTH = 4

HEAD_DIM = 64
N_HEADS = D_MODEL // HEAD_DIM
MOBA_HEADS = N_HEADS // 4
DIL_HEADS = N_HEADS - MOBA_HEADS
SB_HEAD_DIM = 128
SB_HEADS = D_MODEL // SB_HEAD_DIM
MOBA_BLOCK = 256
MOBA_TOPK = 3
DILATED_PAIRS = ((128, 1), (512, 4), (2048, 16))
Q_BLOCK = 128
SB_GROUP_BLOCKS = 8
SB_CUM = 64
FFN_DIM = ((8 * D_MODEL // 3 + 127) // 128) * 128
CONV_WIDTH = 3
ROPE_THETA = 10000.0
LN_EPS = 1e-5
N_EVEN = (DEPTH + 1) // 2
N_ODD = DEPTH // 2
DEEPNORM_ALPHA = (2 * DEPTH) ** 0.25
DEEPNORM_BETA = (8 * DEPTH) ** -0.25

kernel_name = "hybrid_moba_dilated_stickbreak_convffn"


def layer_norm(x, g, b):
    xf = x.astype(jnp.float32)
    mu = jnp.mean(xf, axis=-1, keepdims=True)
    var = jnp.mean(jnp.square(xf - mu), axis=-1, keepdims=True)
    y = (xf - mu) * lax.rsqrt(var + LN_EPS) * g.astype(jnp.float32) + b.astype(jnp.float32)
    return y.astype(x.dtype)


def rope_tables(S):
    inv = 1.0 / (ROPE_THETA ** (jnp.arange(0, HEAD_DIM, 2, dtype=jnp.float32) / HEAD_DIM))
    ang = jnp.arange(S, dtype=jnp.float32)[:, None] * inv[None, :]
    return jnp.cos(ang), jnp.sin(ang)


def apply_rope(x, cos, sin):
    half = HEAD_DIM // 2
    c = cos.astype(x.dtype)
    s = sin.astype(x.dtype)
    x1, x2 = x[..., :half], x[..., half:]
    return jnp.concatenate([x1 * c - x2 * s, x2 * c + x1 * s], axis=-1)


def split_heads(h, w_qkv, n_heads, head_dim):
    B, S, _ = h.shape
    qkv = (h @ w_qkv).reshape(B, S, 3, n_heads, head_dim)
    qkv = qkv.transpose(2, 0, 3, 1, 4)
    return qkv[0], qkv[1], qkv[2]


def merge_heads(o):
    B, H, S, dh = o.shape
    return o.transpose(0, 2, 1, 3).reshape(B, S, H * dh)


def pad_seq(a, Sp):
    S = a.shape[2]
    return jnp.pad(a, ((0, 0), (0, 0), (0, Sp - S), (0, 0)))


def moba_attention(q, k, v):
    B, H, S, dh = q.shape
    Sp = -(-S // MOBA_BLOCK) * MOBA_BLOCK
    q, k, v = pad_seq(q, Sp), pad_seq(k, Sp), pad_seq(v, Sp)
    NB = Sp // MOBA_BLOCK
    kb = k.reshape(B, H, NB, MOBA_BLOCK, dh)
    vb = v.reshape(B, H, NB, MOBA_BLOCK, dh)
    k_sel_count = min(MOBA_TOPK, NB - 1)
    qblk_all = jnp.arange(Sp) // MOBA_BLOCK
    if k_sel_count > 0:
        kmean = jnp.mean(kb.astype(jnp.float32), axis=3).astype(k.dtype)
        gate = jnp.einsum('bhsd,bhnd->bhsn', q, kmean).astype(jnp.float32)
        past = jnp.arange(NB)[None, :] < qblk_all[:, None]
        gate = jnp.where(past, gate, -jnp.inf)
        _, sel = lax.top_k(gate, k_sel_count)
    bi = jnp.arange(B)[:, None, None, None]
    hi = jnp.arange(H)[None, :, None, None]

    def block(c):
        start = c * Q_BLOCK
        qc = lax.dynamic_slice_in_dim(q, start, Q_BLOCK, axis=2)
        pos = start + jnp.arange(Q_BLOCK)
        own = start // MOBA_BLOCK
        k_own = lax.dynamic_index_in_dim(kb, own, axis=2, keepdims=False)
        v_own = lax.dynamic_index_in_dim(vb, own, axis=2, keepdims=False)
        kpos = own * MOBA_BLOCK + jnp.arange(MOBA_BLOCK)
        s_own = jnp.einsum('bhqd,bhkd->bhqk', qc, k_own).astype(jnp.float32)
        s_own = jnp.where(kpos[None, :] <= pos[:, None], s_own, -jnp.inf)
        if k_sel_count == 0:
            p = jax.nn.softmax(s_own, axis=-1)
            return jnp.einsum('bhqk,bhkd->bhqd', p.astype(v.dtype), v_own).astype(q.dtype)
        idx = lax.dynamic_slice_in_dim(sel, start, Q_BLOCK, axis=2)
        k_g = kb[bi, hi, idx]
        v_g = vb[bi, hi, idx]
        s_sel = jnp.einsum('bhqd,bhqnkd->bhqnk', qc, k_g).astype(jnp.float32)
        valid = idx < (pos // MOBA_BLOCK)[None, None, :, None]
        s_sel = jnp.where(valid[..., None], s_sel, -jnp.inf)
        n_sel = k_sel_count * MOBA_BLOCK
        s_all = jnp.concatenate([s_sel.reshape(B, H, Q_BLOCK, n_sel), s_own], axis=-1)
        p = jax.nn.softmax(s_all, axis=-1).astype(v.dtype)
        p_sel = p[..., :n_sel].reshape(B, H, Q_BLOCK, k_sel_count, MOBA_BLOCK)
        o = (jnp.einsum('bhqnk,bhqnkd->bhqd', p_sel, v_g)
             + jnp.einsum('bhqk,bhkd->bhqd', p[..., n_sel:], v_own))
        return o.astype(q.dtype)

    out = lax.map(block, jnp.arange(Sp // Q_BLOCK))
    out = out.transpose(1, 2, 0, 3, 4).reshape(B, H, Sp, dh)
    return out[:, :, :S]


def dilated_branch(q, k, v, window, dil):
    B, H, S, dh = q.shape
    n_back = window // dil
    span = dil * Q_BLOCK
    Sp = -(-S // span) * span
    n_sub = Sp // dil
    nb = n_sub // Q_BLOCK

    def to_sub(a):
        a = pad_seq(a, Sp).reshape(B, H, n_sub, dil, dh).transpose(0, 1, 3, 2, 4)
        return a.reshape(B, H, dil, nb, Q_BLOCK, dh)

    def with_prev(a):
        prev = jnp.pad(a[:, :, :, :-1], ((0, 0), (0, 0), (0, 0), (1, 0), (0, 0), (0, 0)))
        return jnp.concatenate([prev, a], axis=4)

    qs = to_sub(q)
    ks = with_prev(to_sub(k))
    vs = with_prev(to_sub(v))
    s = jnp.einsum('bhrnqd,bhrnkd->bhrnqk', qs, ks).astype(jnp.float32)
    qi = jnp.arange(Q_BLOCK)[:, None]
    kj = jnp.arange(2 * Q_BLOCK)[None, :]
    diff = qi - kj + Q_BLOCK
    nidx = jnp.arange(nb)[:, None, None]
    mask = (diff >= 0) & (diff <= n_back) & ((nidx > 0) | (kj >= Q_BLOCK))
    s = jnp.where(mask, s, -jnp.inf)
    lse = jax.nn.logsumexp(s, axis=-1)
    p = jnp.exp(s - lse[..., None]).astype(v.dtype)
    o = jnp.einsum('bhrnqk,bhrnkd->bhrnqd', p, vs)
    o = o.reshape(B, H, dil, n_sub, dh).transpose(0, 1, 3, 2, 4).reshape(B, H, Sp, dh)[:, :, :S]
    lse = lse.reshape(B, H, dil, n_sub).transpose(0, 1, 3, 2).reshape(B, H, Sp)[:, :, :S]
    return o, lse


def dilated_attention(q, k, v):
    outs, lses = [], []
    for window, dil in DILATED_PAIRS:
        o, l = dilated_branch(q, k, v, window, dil)
        outs.append(o)
        lses.append(l)
    w = jax.nn.softmax(jnp.stack(lses, axis=0), axis=0)
    out = jnp.sum(w[..., None] * jnp.stack(outs, axis=0).astype(jnp.float32), axis=0)
    return out.astype(q.dtype)


def stick_breaking_attention(q, k, v):
    B, H, S, dh = q.shape
    Sp = -(-S // Q_BLOCK) * Q_BLOCK
    q, k, v = pad_seq(q, Sp), pad_seq(k, Sp), pad_seq(v, Sp)
    n_blk = Sp // Q_BLOCK
    ar = jnp.arange(SB_CUM)
    upper = (ar[None, :] >= ar[:, None]).astype(jnp.float32)
    outs = []
    for g0 in range(0, n_blk, SB_GROUP_BLOCKS):
        nb = min(SB_GROUP_BLOCKS, n_blk - g0)
        L = (g0 + nb) * Q_BLOCK
        kg, vg = k[:, :, :L], v[:, :, :L]
        qg = q[:, :, g0 * Q_BLOCK:L]
        kpos = jnp.arange(L)

        def block(c, qg=qg, kg=kg, vg=vg, kpos=kpos, g0=g0, L=L):
            qc = lax.dynamic_slice_in_dim(qg, c * Q_BLOCK, Q_BLOCK, axis=2)
            t = (g0 + c) * Q_BLOCK + jnp.arange(Q_BLOCK)
            z = jnp.einsum('bhqd,bhsd->bhqs', qc, kg).astype(jnp.float32)
            z = jnp.where(kpos[None, :] < t[:, None], z, -jnp.inf)
            sp = jnp.maximum(z, 0.0) + jnp.log1p(jnp.exp(-jnp.abs(z)))
            spb = sp.reshape(B, H, Q_BLOCK, L // SB_CUM, SB_CUM)
            r_in = jnp.einsum('bhqnk,jk->bhqnj', spb, upper)
            tot = r_in[..., 0]
            off = lax.cumsum(tot, axis=3, reverse=True) - tot
            r = (r_in + off[..., None]).reshape(B, H, Q_BLOCK, L)
            a = jnp.exp(z - r).astype(vg.dtype)
            return jnp.einsum('bhqs,bhsd->bhqd', a, vg).astype(q.dtype)

        outs.append(lax.map(block, jnp.arange(nb)))
    out = jnp.concatenate(outs, axis=0).transpose(1, 2, 0, 3, 4).reshape(B, H, Sp, dh)
    return out[:, :, :S]


def even_mixer(h, w_qkv, w_o, cos, sin):
    q, k, v = split_heads(h, w_qkv, N_HEADS, HEAD_DIM)
    q = apply_rope(q, cos, sin) * (HEAD_DIM ** -0.5)
    k = apply_rope(k, cos, sin)
    oa = moba_attention(q[:, :MOBA_HEADS], k[:, :MOBA_HEADS], v[:, :MOBA_HEADS])
    ob = dilated_attention(q[:, MOBA_HEADS:], k[:, MOBA_HEADS:], v[:, MOBA_HEADS:])
    o = jnp.concatenate([oa, ob], axis=1)
    return merge_heads(o) @ w_o


def odd_mixer(h, w_qkv, w_o):
    q, k, v = split_heads(h, w_qkv, SB_HEADS, SB_HEAD_DIM)
    o = stick_breaking_attention(q * (SB_HEAD_DIM ** -0.5), k, v)
    return merge_heads(o) @ w_o


def conv_ffn(h, w_up, conv_w, conv_b, w_down):
    S = h.shape[1]
    u = h @ w_up
    up = jnp.pad(u, ((0, 0), (CONV_WIDTH - 1, 0), (0, 0)))
    c = conv_b
    for j in range(CONV_WIDTH):
        c = c + conv_w[j] * up[:, j:j + S]
    g, val = c[..., :FFN_DIM], c[..., FFN_DIM:]
    return (jax.nn.silu(g) * val) @ w_down


def setup_inputs(seed: int = 0) -> dict:
    key = jax.random.key(seed)
    ks = jax.random.split(key, 20)
    D, F = D_MODEL, FFN_DIM
    sd = D ** -0.5

    def qkv_weight(k0, n):
        a, b, c = jax.random.split(k0, 3)
        wq = jax.random.normal(a, (n, D, D), jnp.float32) * sd
        wk = jax.random.normal(b, (n, D, D), jnp.float32) * sd
        wv = jax.random.normal(c, (n, D, D), jnp.float32) * sd * DEEPNORM_BETA
        return jnp.concatenate([wq, wk, wv], axis=-1)

    return {
        "x": jax.random.normal(ks[0], (BATCH, SEQ, D), jnp.float32),
        "w_qkv_ab": qkv_weight(ks[1], N_EVEN),
        "w_o_ab": jax.random.normal(ks[2], (N_EVEN, D, D), jnp.float32) * sd * DEEPNORM_BETA,
        "w_qkv_c": qkv_weight(ks[3], N_ODD),
        "w_o_c": jax.random.normal(ks[4], (N_ODD, D, D), jnp.float32) * sd * DEEPNORM_BETA,
        "ln_mix_g": 1.0 + 0.02 * jax.random.normal(ks[5], (DEPTH, D), jnp.float32),
        "ln_mix_b": 0.02 * jax.random.normal(ks[6], (DEPTH, D), jnp.float32),
        "w_up": jax.random.normal(ks[7], (DEPTH, D, 2 * F), jnp.float32) * sd * DEEPNORM_BETA,
        "conv_w": jax.random.normal(ks[8], (DEPTH, CONV_WIDTH, 2 * F), jnp.float32) * CONV_WIDTH ** -0.5,
        "conv_b": 0.01 * jax.random.normal(ks[9], (DEPTH, 2 * F), jnp.float32),
        "w_down": jax.random.normal(ks[10], (DEPTH, F, D), jnp.float32) * F ** -0.5 * DEEPNORM_BETA,
        "ln_ffn_g": 1.0 + 0.02 * jax.random.normal(ks[11], (DEPTH, D), jnp.float32),
        "ln_ffn_b": 0.02 * jax.random.normal(ks[12], (DEPTH, D), jnp.float32),
    }


def reference(x, w_qkv_ab, w_o_ab, w_qkv_c, w_o_c, ln_mix_g, ln_mix_b,
              w_up, conv_w, conv_b, w_down, ln_ffn_g, ln_ffn_b):
    cos, sin = rope_tables(x.shape[1])
    for i in range(DEPTH):
        if i % 2 == 0:
            m = even_mixer(x, w_qkv_ab[i // 2], w_o_ab[i // 2], cos, sin)
        else:
            m = odd_mixer(x, w_qkv_c[i // 2], w_o_c[i // 2])
        x = layer_norm(DEEPNORM_ALPHA * x + m.astype(x.dtype), ln_mix_g[i], ln_mix_b[i])
        f = conv_ffn(x, w_up[i], conv_w[i], conv_b[i], w_down[i])
        x = layer_norm(DEEPNORM_ALPHA * x + f.astype(x.dtype), ln_ffn_g[i], ln_ffn_b[i])
    return x
```

```python
import functools

import numpy as np
import jax
import jax.numpy as jnp
from jax import lax
from jax.experimental import pallas as pl
from jax.experimental.pallas import tpu as pltpu

D_MODEL = 1024
DEPTH = 4
HEAD_DIM = 64
N_HEADS = D_MODEL // HEAD_DIM
MOBA_HEADS = N_HEADS // 4
SB_HEAD_DIM = 128
SB_HEADS = D_MODEL // SB_HEAD_DIM
MOBA_BLOCK = 256
MOBA_TOPK = 3
DILATED_PAIRS = ((128, 1), (512, 4), (2048, 16))
FFN_DIM = ((8 * D_MODEL // 3 + 127) // 128) * 128
CONV_WIDTH = 3
ROPE_THETA = 10000.0
LN_EPS = 1e-5
DEEPNORM_ALPHA = (2 * DEPTH) ** 0.25

LANES = 128
SUBLANES = 8
VMEM_LIMIT_BYTES = 56 * 1024 * 1024

ROW_TILE = 512
FFN_ROW_TILE = 1024
FFN_COL_TILE = 256
ATT_TILE = 256
MAX_DILATED_WINDOW = max(w for w, _ in DILATED_PAIRS)
DILATED_KEY_TILES = MAX_DILATED_WINDOW // ATT_TILE + 1

NEG = -1e30
F32 = jnp.float32
BF16 = jnp.bfloat16

_CONTRACT_LANES = (((1,), (1,)), ((), ()))


def _params(*semantics):
    return pltpu.CompilerParams(dimension_semantics=semantics, vmem_limit_bytes=VMEM_LIMIT_BYTES)


def _layer_norm(y, g, b):
    mu = jnp.mean(y, axis=-1, keepdims=True)
    yc = y - mu
    var = jnp.mean(yc * yc, axis=-1, keepdims=True)
    return yc * lax.rsqrt(var + LN_EPS) * g + b


def _qkv_kernel(*refs, rope, q_scale):
    if rope:
        x_ref, w_ref, cos_ref, sin_ref, o_ref = refs
    else:
        x_ref, w_ref, o_ref = refs
    xb = x_ref[...].astype(BF16)
    for part in range(3):
        y = jnp.dot(xb, w_ref[:, part * D_MODEL:(part + 1) * D_MODEL], preferred_element_type=F32)
        if rope and part < 2:
            c = cos_ref[...]
            s = sin_ref[...]
            lane = lax.broadcasted_iota(jnp.int32, c.shape, 1)
            first_half = (lane % HEAD_DIM) < (HEAD_DIM // 2)
            for cb in range(D_MODEL // LANES):
                yc = y[:, cb * LANES:(cb + 1) * LANES]
                partner = jnp.where(first_half,
                                    pltpu.roll(yc, LANES - HEAD_DIM // 2, 1),
                                    pltpu.roll(yc, HEAD_DIM // 2, 1))
                yc = yc * c + partner * s
                if part == 0:
                    yc = yc * q_scale
                o_ref[:, part * D_MODEL + cb * LANES:part * D_MODEL + (cb + 1) * LANES] = yc.astype(BF16)
        else:
            if part == 0:
                y = y * q_scale
            o_ref[:, part * D_MODEL:(part + 1) * D_MODEL] = y.astype(BF16)


def _qkv_proj(x, w_bf16, q_scale, rope_tables=None):
    S = x.shape[0]
    rope = rope_tables is not None
    in_specs = [pl.BlockSpec((ROW_TILE, D_MODEL), lambda i: (i, 0)),
                pl.BlockSpec((D_MODEL, 3 * D_MODEL), lambda i: (0, 0))]
    args = [x, w_bf16]
    if rope:
        in_specs += [pl.BlockSpec((ROW_TILE, LANES), lambda i: (i, 0))] * 2
        args += list(rope_tables)
    return pl.pallas_call(
        functools.partial(_qkv_kernel, rope=rope, q_scale=q_scale),
        grid=(S // ROW_TILE,),
        in_specs=in_specs,
        out_specs=pl.BlockSpec((ROW_TILE, 3 * D_MODEL), lambda i: (i, 0)),
        out_shape=jax.ShapeDtypeStruct((S, 3 * D_MODEL), BF16),
        compiler_params=_params("parallel"),
        name="qkv_proj",
    )(*args)


def _rope_tables(S):
    inv = 1.0 / (ROPE_THETA ** (jnp.arange(0, HEAD_DIM, 2, dtype=F32) / HEAD_DIM))
    ang = jnp.arange(S, dtype=F32)[:, None] * inv[None, :]
    cos, sin = jnp.cos(ang), jnp.sin(ang)
    cos_t = jnp.tile(jnp.concatenate([cos, cos], axis=-1), (1, LANES // HEAD_DIM))
    sin_t = jnp.tile(jnp.concatenate([-sin, sin], axis=-1), (1, LANES // HEAD_DIM))
    return cos_t, sin_t


def _moba_select_kernel(q_ref, k_ref, qa_ref, ka_ref, km_ref):
    i = pl.program_id(1)
    n_blocks = k_ref.shape[0] // MOBA_BLOCK

    @pl.when(i == 0)
    def _():
        km_ref[...] = jnp.zeros_like(km_ref)

        def body(b, carry):
            kb = k_ref[pl.ds(pl.multiple_of(b * MOBA_BLOCK, MOBA_BLOCK), MOBA_BLOCK), :].astype(F32)
            km_ref[pl.ds(b, 1), :] = jnp.sum(kb, axis=0, keepdims=True) * (1.0 / MOBA_BLOCK)
            return carry

        lax.fori_loop(0, n_blocks, body, 0)

    q = q_ref[...].astype(F32)
    kt = k_ref[pl.ds(pl.multiple_of(i * ATT_TILE, ATT_TILE), ATT_TILE), :].astype(F32)
    lane_i = lax.broadcasted_iota(jnp.int32, q.shape, 1)
    lane = lane_i.astype(F32)
    low = lane_i < HEAD_DIM
    blk = i.astype(F32)
    km = km_ref[...]
    onehot = jnp.where(lane_i - HEAD_DIM == i, 1.0, 0.0)
    q_sw = pltpu.roll(q, HEAD_DIM, 1)
    k_sw = pltpu.roll(kt, HEAD_DIM, 1)
    for h in range(2):
        qh = jnp.where(low if h == 0 else jnp.logical_not(low), q, 0.0)
        gate = lax.dot_general(qh, km, _CONTRACT_LANES, precision=lax.Precision.HIGHEST,
                               preferred_element_type=F32)
        g = jnp.where(lane < blk, gate, -jnp.inf)
        sel = lane == blk
        for _ in range(MOBA_TOPK):
            m = jnp.max(g, axis=1, keepdims=True)
            first = jnp.min(jnp.where(g == m, lane, float(LANES)), axis=1, keepdims=True)
            pick = jnp.logical_and(lane == first, m > -jnp.inf)
            sel = jnp.logical_or(sel, pick)
            g = jnp.where(pick, -jnp.inf, g)
        bias = pltpu.roll(jnp.where(sel, 0.0, NEG), HEAD_DIM, 1)
        qa_ref[h] = jnp.where(low, q if h == 0 else q_sw, bias).astype(BF16)
        ka_ref[h] = jnp.where(low, kt if h == 0 else k_sw, onehot).astype(BF16)


def _moba_flash_kernel(qa_ref, ka_ref, v_ref, o_ref, m_ref, l_ref, acc_ref):
    i = pl.program_id(1)
    m_ref[...] = jnp.full_like(m_ref, -jnp.inf)
    l_ref[...] = jnp.zeros_like(l_ref)
    acc_ref[...] = jnp.zeros_like(acc_ref)
    row = lax.broadcasted_iota(jnp.int32, (ATT_TILE, ATT_TILE), 0)
    col = lax.broadcasted_iota(jnp.int32, (ATT_TILE, ATT_TILE), 1)

    def step(j, diagonal):
        off = pl.multiple_of(j * ATT_TILE, ATT_TILE)
        v = v_ref[pl.ds(off, ATT_TILE), :]
        for h in range(2):
            s = lax.dot_general(qa_ref[h], ka_ref[h, pl.ds(off, ATT_TILE), :], _CONTRACT_LANES,
                                preferred_element_type=F32)
            if diagonal:
                s = jnp.where(col <= row, s, -jnp.inf)
            m_old = m_ref[h]
            m_new = jnp.maximum(m_old, jnp.max(s, axis=1, keepdims=True))
            p = jnp.exp(s - m_new)
            alpha = jnp.exp(m_old - m_new)
            l_ref[h] = alpha * l_ref[h] + jnp.sum(p, axis=1, keepdims=True)
            acc_ref[h] = alpha * acc_ref[h] + jnp.dot(p.astype(BF16), v, preferred_element_type=F32)
            m_ref[h] = m_new

    def body(j, carry):
        step(j, False)
        return carry

    lax.fori_loop(0, i, body, 0)
    step(i, True)
    lane = lax.broadcasted_iota(jnp.int32, (ATT_TILE, LANES), 1)
    o_ref[...] = jnp.where(lane < HEAD_DIM, acc_ref[0] / l_ref[0], acc_ref[1] / l_ref[1]).astype(BF16)


def _moba_attention(qkv):
    S = qkv.shape[0]
    n_pairs = MOBA_HEADS // 2
    k_col0 = D_MODEL // LANES
    v_col0 = 2 * D_MODEL // LANES
    aug = jax.ShapeDtypeStruct((MOBA_HEADS, S, LANES), BF16)
    qa, ka = pl.pallas_call(
        _moba_select_kernel,
        grid=(n_pairs, S // ATT_TILE),
        in_specs=[pl.BlockSpec((ATT_TILE, LANES), lambda p, i: (i, p)),
                  pl.BlockSpec((S, LANES), lambda p, i: (0, k_col0 + p))],
        out_specs=[pl.BlockSpec((2, ATT_TILE, LANES), lambda p, i: (p, i, 0))] * 2,
        out_shape=[aug, aug],
        scratch_shapes=[pltpu.VMEM((LANES, LANES), F32)],
        compiler_params=_params("parallel", "arbitrary"),
        name="moba_select",
    )(qkv, qkv)
    return pl.pallas_call(
        _moba_flash_kernel,
        grid=(n_pairs, S // ATT_TILE),
        in_specs=[pl.BlockSpec((2, ATT_TILE, LANES), lambda p, i: (p, i, 0)),
                  pl.BlockSpec((2, S, LANES), lambda p, i: (p, 0, 0)),
                  pl.BlockSpec((S, LANES), lambda p, i: (0, v_col0 + p))],
        out_specs=pl.BlockSpec((ATT_TILE, LANES), lambda p, i: (i, p)),
        out_shape=jax.ShapeDtypeStruct((S, n_pairs * LANES), BF16),
        scratch_shapes=[pltpu.VMEM((2, ATT_TILE, 1), F32), pltpu.VMEM((2, ATT_TILE, 1), F32),
                        pltpu.VMEM((2, ATT_TILE, LANES), F32)],
        compiler_params=_params("parallel", "parallel"),
        name="moba_flash",
    )(qa, ka, qkv)


def _dilated_bias_table():
    row = np.arange(ATT_TILE)[:, None]
    col = np.arange(ATT_TILE)[None, :]
    tiles = []
    for t in range(DILATED_KEY_TILES):
        d = row - col + ATT_TILE * (DILATED_KEY_TILES - 1 - t)
        count = np.zeros_like(d)
        for window, dil in DILATED_PAIRS:
            count += ((d >= 0) & (d <= window) & (d % dil == 0)).astype(d.dtype)
        tiles.append(np.where(count > 0, np.log(np.maximum(count, 1)), NEG))
    return jnp.asarray(np.stack(tiles), dtype=F32)


def _dilated_kernel(q_ref, k_ref, v_ref, tab_ref, o_ref, m_ref, l_ref, acc_ref):
    i = pl.program_id(1)
    m_ref[...] = jnp.full_like(m_ref, -jnp.inf)
    l_ref[...] = jnp.zeros_like(l_ref)
    acc_ref[...] = jnp.zeros_like(acc_ref)
    q = q_ref[...]
    lane = lax.broadcasted_iota(jnp.int32, q.shape, 1)
    low = lane < HEAD_DIM
    zero = jnp.zeros_like(q)
    q_heads = (jnp.where(low, q, zero), jnp.where(low, zero, q))
    last = DILATED_KEY_TILES - 1

    def body(t, carry):
        off = pl.multiple_of((i - last + t) * ATT_TILE, ATT_TILE)
        k = k_ref[pl.ds(off, ATT_TILE), :]
        v = v_ref[pl.ds(off, ATT_TILE), :]
        bias = tab_ref[t]
        for h in range(2):
            s = lax.dot_general(q_heads[h], k, _CONTRACT_LANES, preferred_element_type=F32) + bias
            m_old = m_ref[h]
            m_new = jnp.maximum(m_old, jnp.max(s, axis=1, keepdims=True))
            p = jnp.exp(s - m_new)
            alpha = jnp.exp(m_old - m_new)
            l_ref[h] = alpha * l_ref[h] + jnp.sum(p, axis=1, keepdims=True)
            acc_ref[h] = alpha * acc_ref[h] + jnp.dot(p.astype(BF16), v, preferred_element_type=F32)
            m_ref[h] = m_new
        return carry

    lax.fori_loop(jnp.maximum(last - i, 0), DILATED_KEY_TILES, body, 0)
    o_ref[...] = jnp.where(low, acc_ref[0] / l_ref[0], acc_ref[1] / l_ref[1]).astype(BF16)


def _dilated_attention(qkv):
    S = qkv.shape[0]
    n_pairs = (N_HEADS - MOBA_HEADS) // 2
    q_col0 = MOBA_HEADS // 2
    k_col0 = D_MODEL // LANES + q_col0
    v_col0 = 2 * D_MODEL // LANES + q_col0
    return pl.pallas_call(
        _dilated_kernel,
        grid=(n_pairs, S // ATT_TILE),
        in_specs=[pl.BlockSpec((ATT_TILE, LANES), lambda p, i: (i, q_col0 + p)),
                  pl.BlockSpec((S, LANES), lambda p, i: (0, k_col0 + p)),
                  pl.BlockSpec((S, LANES), lambda p, i: (0, v_col0 + p)),
                  pl.BlockSpec((DILATED_KEY_TILES, ATT_TILE, ATT_TILE), lambda p, i: (0, 0, 0))],
        out_specs=pl.BlockSpec((ATT_TILE, LANES), lambda p, i: (i, p)),
        out_shape=jax.ShapeDtypeStruct((S, n_pairs * LANES), BF16),
        scratch_shapes=[pltpu.VMEM((2, ATT_TILE, 1), F32), pltpu.VMEM((2, ATT_TILE, 1), F32),
                        pltpu.VMEM((2, ATT_TILE, LANES), F32)],
        compiler_params=_params("parallel", "parallel"),
        name="dilated_attention",
    )(qkv, qkv, qkv, _dilated_bias_table())


def _sb_kernel(q_ref, k_ref, v_ref, o_ref, off_ref, acc_ref):
    i = pl.program_id(1)
    q = q_ref[...]
    row = lax.broadcasted_iota(jnp.int32, (ATT_TILE, ATT_TILE), 0)
    col = lax.broadcasted_iota(jnp.int32, (ATT_TILE, ATT_TILE), 1)
    suffix = jnp.where(row >= col, 1.0, 0.0).astype(BF16)
    acc_ref[...] = jnp.zeros_like(acc_ref)
    off_ref[...] = jnp.zeros_like(off_ref)

    def tile(j, diagonal):
        start = pl.multiple_of(j * ATT_TILE, ATT_TILE)
        k = k_ref[pl.ds(start, ATT_TILE), :]
        v = v_ref[pl.ds(start, ATT_TILE), :]
        z = lax.dot_general(q, k, _CONTRACT_LANES, preferred_element_type=F32)
        sp = jnp.maximum(z, 0.0) + jnp.log(1.0 + jnp.exp(-jnp.abs(z)))
        if diagonal:
            valid = col < row
            sp = jnp.where(valid, sp, 0.0)
        sp_hi = sp.astype(BF16)
        sp_lo = (sp - sp_hi.astype(F32)).astype(BF16)
        r_in = (jnp.dot(sp_hi, suffix, preferred_element_type=F32)
                + jnp.dot(sp_lo, suffix, preferred_element_type=F32))
        a = jnp.exp(z - (r_in + off_ref[...]))
        if diagonal:
            a = jnp.where(valid, a, 0.0)
        acc_ref[...] += jnp.dot(a.astype(BF16), v, preferred_element_type=F32)
        off_ref[...] += r_in[:, 0:1]

    tile(i, True)

    def body(n, carry):
        tile(i - 1 - n, False)
        return carry

    lax.fori_loop(0, i, body, 0)
    o_ref[...] = acc_ref[...].astype(BF16)


def _stick_breaking_attention(qkv):
    S = qkv.shape[0]
    k_col0 = D_MODEL // LANES
    v_col0 = 2 * D_MODEL // LANES
    return pl.pallas_call(
        _sb_kernel,
        grid=(SB_HEADS, S // ATT_TILE),
        in_specs=[pl.BlockSpec((ATT_TILE, LANES), lambda h, i: (i, h)),
                  pl.BlockSpec((S, LANES), lambda h, i: (0, k_col0 + h)),
                  pl.BlockSpec((S, LANES), lambda h, i: (0, v_col0 + h))],
        out_specs=pl.BlockSpec((ATT_TILE, LANES), lambda h, i: (i, h)),
        out_shape=jax.ShapeDtypeStruct((S, D_MODEL), BF16),
        scratch_shapes=[pltpu.VMEM((ATT_TILE, 1), F32), pltpu.VMEM((ATT_TILE, LANES), F32)],
        compiler_params=_params("parallel", "parallel"),
        name="stick_breaking",
    )(qkv, qkv, qkv)


def _proj_ln_kernel(*refs, n_parts):
    o_refs = refs[:n_parts]
    w_refs = refs[n_parts:2 * n_parts]
    x_ref, g_ref, b_ref, out_ref = refs[2 * n_parts:]
    m = jnp.dot(o_refs[0][...], w_refs[0][...], preferred_element_type=F32)
    for o_ref, w_ref in zip(o_refs[1:], w_refs[1:]):
        m = m + jnp.dot(o_ref[...], w_ref[...], preferred_element_type=F32)
    out_ref[...] = _layer_norm(DEEPNORM_ALPHA * x_ref[...] + m, g_ref[...], b_ref[...])


def _proj_ln(o_parts, w_parts, x, g, b):
    S = x.shape[0]
    n = len(o_parts)
    in_specs = ([pl.BlockSpec((ROW_TILE, o.shape[1]), lambda i: (i, 0)) for o in o_parts]
                + [pl.BlockSpec(w.shape, lambda i: (0, 0)) for w in w_parts]
                + [pl.BlockSpec((ROW_TILE, D_MODEL), lambda i: (i, 0)),
                   pl.BlockSpec((1, D_MODEL), lambda i: (0, 0)),
                   pl.BlockSpec((1, D_MODEL), lambda i: (0, 0))])
    return pl.pallas_call(
        functools.partial(_proj_ln_kernel, n_parts=n),
        grid=(S // ROW_TILE,),
        in_specs=in_specs,
        out_specs=pl.BlockSpec((ROW_TILE, D_MODEL), lambda i: (i, 0)),
        out_shape=jax.ShapeDtypeStruct((S, D_MODEL), F32),
        compiler_params=_params("parallel"),
        name="out_proj_ln",
    )(*o_parts, *w_parts, x, g.reshape(1, D_MODEL), b.reshape(1, D_MODEL))


def _ffn_kernel(x_ref, wg_ref, wv_ref, cwg_ref, cwv_ref, cbg_ref, cbv_ref, wd_ref, g_ref, b_ref,
                out_ref, xb_ref, acc_ref, ug_ref, uv_ref, carry_ref):
    i = pl.program_id(0)
    f = pl.program_id(1)
    rows = x_ref.shape[0]

    @pl.when(f == 0)
    def _():
        xb_ref[...] = x_ref[...].astype(BF16)
        acc_ref[...] = jnp.zeros_like(acc_ref)

    @pl.when(i == 0)
    def _():
        carry_ref[f] = jnp.zeros(carry_ref.shape[1:], F32)

    xb = xb_ref[...]

    def conv(w_ref, cw_ref, cb_ref, u_ref, part):
        u = jnp.dot(xb, w_ref[...], preferred_element_type=F32)
        u_ref[0:SUBLANES, :] = carry_ref[f, part]
        u_ref[SUBLANES:SUBLANES + rows, :] = u
        carry_ref[f, part] = u[rows - SUBLANES:rows, :]
        cw = cw_ref[...]
        c = cb_ref[...]
        for tap in range(CONV_WIDTH - 1):
            shift = CONV_WIDTH - 1 - tap
            c = c + cw[tap:tap + 1, :] * u_ref[SUBLANES - shift:SUBLANES - shift + rows, :]
        return c + cw[CONV_WIDTH - 1:CONV_WIDTH, :] * u

    cg = conv(wg_ref, cwg_ref, cbg_ref, ug_ref, 0)
    cv = conv(wv_ref, cwv_ref, cbv_ref, uv_ref, 1)
    h = cg * (1.0 / (1.0 + jnp.exp(-cg))) * cv
    acc_ref[...] += jnp.dot(h.astype(BF16), wd_ref[...], preferred_element_type=F32)

    @pl.when(f == pl.num_programs(1) - 1)
    def _():
        out_ref[...] = _layer_norm(DEEPNORM_ALPHA * x_ref[...] + acc_ref[...], g_ref[...], b_ref[...])


def _conv_ffn_ln(x, w_up, conv_w, conv_b, w_down, g, b):
    S = x.shape[0]
    n_f = FFN_DIM // FFN_COL_TILE
    tm, tf = FFN_ROW_TILE, FFN_COL_TILE
    conv_b = conv_b.reshape(1, 2 * FFN_DIM)
    return pl.pallas_call(
        _ffn_kernel,
        grid=(S // tm, n_f),
        in_specs=[pl.BlockSpec((tm, D_MODEL), lambda i, f: (i, 0)),
                  pl.BlockSpec((D_MODEL, tf), lambda i, f: (0, f)),
                  pl.BlockSpec((D_MODEL, tf), lambda i, f: (0, n_f + f)),
                  pl.BlockSpec((CONV_WIDTH, tf), lambda i, f: (0, f)),
                  pl.BlockSpec((CONV_WIDTH, tf), lambda i, f: (0, n_f + f)),
                  pl.BlockSpec((1, tf), lambda i, f: (0, f)),
                  pl.BlockSpec((1, tf), lambda i, f: (0, n_f + f)),
                  pl.BlockSpec((tf, D_MODEL), lambda i, f: (f, 0)),
                  pl.BlockSpec((1, D_MODEL), lambda i, f: (0, 0)),
                  pl.BlockSpec((1, D_MODEL), lambda i, f: (0, 0))],
        out_specs=pl.BlockSpec((tm, D_MODEL), lambda i, f: (i, 0)),
        out_shape=jax.ShapeDtypeStruct((S, D_MODEL), F32),
        scratch_shapes=[pltpu.VMEM((tm, D_MODEL), BF16),
                        pltpu.VMEM((tm, D_MODEL), F32),
                        pltpu.VMEM((tm + SUBLANES, tf), F32),
                        pltpu.VMEM((tm + SUBLANES, tf), F32),
                        pltpu.VMEM((n_f, 2, SUBLANES, tf), F32)],
        compiler_params=_params("arbitrary", "arbitrary"),
        name="conv_ffn_ln",
    )(x, w_up, w_up, conv_w, conv_w, conv_b, conv_b, w_down,
      g.reshape(1, D_MODEL), b.reshape(1, D_MODEL))


def kernel(x, w_qkv_ab, w_o_ab, w_qkv_c, w_o_c, ln_mix_g, ln_mix_b, w_up, conv_w, conv_b, w_down,
           ln_ffn_g, ln_ffn_b):
    batch, S, _ = x.shape
    rope_tables = _rope_tables(S)
    moba_width = MOBA_HEADS * HEAD_DIM
    outs = []
    for bi in range(batch):
        h = x[bi]
        for layer in range(DEPTH):
            if layer % 2 == 0:
                qkv = _qkv_proj(h, w_qkv_ab[layer // 2].astype(BF16), HEAD_DIM ** -0.5, rope_tables)
                w_o = w_o_ab[layer // 2].astype(BF16)
                o_parts = [_moba_attention(qkv), _dilated_attention(qkv)]
                w_parts = [w_o[:moba_width], w_o[moba_width:]]
            else:
                qkv = _qkv_proj(h, w_qkv_c[layer // 2].astype(BF16), SB_HEAD_DIM ** -0.5)
                o_parts = [_stick_breaking_attention(qkv)]
                w_parts = [w_o_c[layer // 2].astype(BF16)]
            h = _proj_ln(o_parts, w_parts, h, ln_mix_g[layer], ln_mix_b[layer])
            h = _conv_ffn_ln(h, w_up[layer].astype(BF16), conv_w[layer], conv_b[layer],
                             w_down[layer].astype(BF16), ln_ffn_g[layer], ln_ffn_b[layer])
        outs.append(h)
    return jnp.stack(outs)
```

```python
import functools

import numpy as np
import jax
import jax.numpy as jnp
from jax import lax
from jax.experimental import pallas as pl
from jax.experimental.pallas import tpu as pltpu

D_MODEL = 1024
DEPTH = 4
HEAD_DIM = 64
N_HEADS = D_MODEL // HEAD_DIM
MOBA_HEADS = N_HEADS // 4
SB_HEAD_DIM = 128
SB_HEADS = D_MODEL // SB_HEAD_DIM
MOBA_BLOCK = 256
MOBA_TOPK = 3
DILATED_PAIRS = ((128, 1), (512, 4), (2048, 16))
FFN_DIM = ((8 * D_MODEL // 3 + 127) // 128) * 128
CONV_WIDTH = 3
ROPE_THETA = 10000.0
LN_EPS = 1e-5
DEEPNORM_ALPHA = (2 * DEPTH) ** 0.25

LANES = 128
SUBLANES = 8
VMEM_LIMIT_BYTES = 56 * 1024 * 1024

ROW_TILE = 512
FFN_ROW_TILE = 1024
FFN_COL_TILE = 256
ATT_TILE = 256
MAX_DILATED_WINDOW = max(w for w, _ in DILATED_PAIRS)
DILATED_KEY_TILES = MAX_DILATED_WINDOW // ATT_TILE + 1

NEG = -1e30
SB_UNDERFLOW = 104.0
F32 = jnp.float32
BF16 = jnp.bfloat16

_CONTRACT_LANES = (((1,), (1,)), ((), ()))


def _params(*semantics):
    return pltpu.CompilerParams(dimension_semantics=semantics, vmem_limit_bytes=VMEM_LIMIT_BYTES)


def _layer_norm(y, g, b):
    mu = jnp.mean(y, axis=-1, keepdims=True)
    yc = y - mu
    var = jnp.mean(yc * yc, axis=-1, keepdims=True)
    return yc * lax.rsqrt(var + LN_EPS) * g + b


def _qkv_kernel(*refs, rope, q_scale):
    if rope:
        x_ref, w_ref, wvt_ref, cos_ref, sin_ref, qk_ref, vt_ref = refs
    else:
        x_ref, w_ref, wvt_ref, qk_ref, vt_ref = refs
    xb = x_ref[...].astype(BF16)
    for part in range(2):
        y = jnp.dot(xb, w_ref[:, part * D_MODEL:(part + 1) * D_MODEL], preferred_element_type=F32)
        if rope:
            c = cos_ref[...]
            s = sin_ref[...]
            lane = lax.broadcasted_iota(jnp.int32, c.shape, 1)
            first_half = (lane % HEAD_DIM) < (HEAD_DIM // 2)
            for cb in range(D_MODEL // LANES):
                yc = y[:, cb * LANES:(cb + 1) * LANES]
                partner = jnp.where(first_half,
                                    pltpu.roll(yc, LANES - HEAD_DIM // 2, 1),
                                    pltpu.roll(yc, HEAD_DIM // 2, 1))
                yc = yc * c + partner * s
                if part == 0:
                    yc = yc * q_scale
                qk_ref[:, part * D_MODEL + cb * LANES:part * D_MODEL + (cb + 1) * LANES] = yc.astype(BF16)
        else:
            if part == 0:
                y = y * q_scale
            qk_ref[:, part * D_MODEL:(part + 1) * D_MODEL] = y.astype(BF16)
    vt = lax.dot_general(wvt_ref[...], xb, _CONTRACT_LANES,
                         preferred_element_type=F32).astype(BF16)
    for t in range(ROW_TILE // ATT_TILE):
        vt_ref[t] = vt[:, t * ATT_TILE:(t + 1) * ATT_TILE]


def _qkv_proj(x, w_qkv, q_scale, rope_tables=None):
    w_qk = w_qkv[:, :2 * D_MODEL].astype(BF16)
    w_vt = w_qkv[:, 2 * D_MODEL:].T.astype(BF16)
    S = x.shape[0]
    rope = rope_tables is not None
    tiles = ROW_TILE // ATT_TILE
    in_specs = [pl.BlockSpec((ROW_TILE, D_MODEL), lambda i: (i, 0)),
                pl.BlockSpec((D_MODEL, 2 * D_MODEL), lambda i: (0, 0)),
                pl.BlockSpec((D_MODEL, D_MODEL), lambda i: (0, 0))]
    args = [x, w_qk, w_vt]
    if rope:
        in_specs += [pl.BlockSpec((ROW_TILE, LANES), lambda i: (i, 0))] * 2
        args += list(rope_tables)
    return pl.pallas_call(
        functools.partial(_qkv_kernel, rope=rope, q_scale=q_scale),
        grid=(S // ROW_TILE,),
        in_specs=in_specs,
        out_specs=[pl.BlockSpec((ROW_TILE, 2 * D_MODEL), lambda i: (i, 0)),
                   pl.BlockSpec((tiles, D_MODEL, ATT_TILE), lambda i: (i, 0, 0))],
        out_shape=[jax.ShapeDtypeStruct((S, 2 * D_MODEL), BF16),
                   jax.ShapeDtypeStruct((S // ATT_TILE, D_MODEL, ATT_TILE), BF16)],
        compiler_params=_params("parallel"),
        name="qkv_proj",
    )(*args)


def _rope_tables(S):
    inv = 1.0 / (ROPE_THETA ** (jnp.arange(0, HEAD_DIM, 2, dtype=F32) / HEAD_DIM))
    ang = jnp.arange(S, dtype=F32)[:, None] * inv[None, :]
    cos, sin = jnp.cos(ang), jnp.sin(ang)
    cos_t = jnp.tile(jnp.concatenate([cos, cos], axis=-1), (1, LANES // HEAD_DIM))
    sin_t = jnp.tile(jnp.concatenate([-sin, sin], axis=-1), (1, LANES // HEAD_DIM))
    return cos_t, sin_t


def _softmax_step(s_t, m_old, l_old):
    m_new = jnp.maximum(m_old, jnp.max(s_t, axis=0, keepdims=True))
    p_t = jnp.exp(s_t - m_new)
    alpha = jnp.exp(m_old - m_new)
    l_new = alpha * l_old + jnp.sum(p_t, axis=0, keepdims=True)
    return p_t.astype(BF16), alpha, m_new, l_new


def _moba_select_kernel(q_ref, k_ref, qa_ref, ka_ref, km_ref):
    i = pl.program_id(1)
    n_blocks = k_ref.shape[0] // MOBA_BLOCK

    @pl.when(i == 0)
    def _():
        km_ref[...] = jnp.zeros_like(km_ref)

        def body(b, carry):
            kb = k_ref[pl.ds(pl.multiple_of(b * MOBA_BLOCK, MOBA_BLOCK), MOBA_BLOCK), :].astype(F32)
            km_ref[pl.ds(b, 1), :] = jnp.sum(kb, axis=0, keepdims=True) * (1.0 / MOBA_BLOCK)
            return carry

        lax.fori_loop(0, n_blocks, body, 0)

    q = q_ref[...].astype(F32)
    kt = k_ref[pl.ds(pl.multiple_of(i * ATT_TILE, ATT_TILE), ATT_TILE), :].astype(F32)
    lane_i = lax.broadcasted_iota(jnp.int32, q.shape, 1)
    lane = lane_i.astype(F32)
    low = lane_i < HEAD_DIM
    blk = i.astype(F32)
    km = km_ref[...]
    onehot = jnp.where(lane_i - HEAD_DIM == i, 1.0, 0.0)
    q_sw = pltpu.roll(q, HEAD_DIM, 1)
    k_sw = pltpu.roll(kt, HEAD_DIM, 1)
    for h in range(2):
        qh = jnp.where(low if h == 0 else jnp.logical_not(low), q, 0.0)
        gate = lax.dot_general(qh, km, _CONTRACT_LANES, precision=lax.Precision.HIGHEST,
                               preferred_element_type=F32)
        g = jnp.where(lane < blk, gate, -jnp.inf)
        sel = lane == blk
        for _ in range(MOBA_TOPK):
            m = jnp.max(g, axis=1, keepdims=True)
            first = jnp.min(jnp.where(g == m, lane, float(LANES)), axis=1, keepdims=True)
            pick = jnp.logical_and(lane == first, m > -jnp.inf)
            sel = jnp.logical_or(sel, pick)
            g = jnp.where(pick, -jnp.inf, g)
        bias = pltpu.roll(jnp.where(sel, 0.0, NEG), HEAD_DIM, 1)
        qa_ref[h] = jnp.where(low, q if h == 0 else q_sw, bias).astype(BF16)
        ka_ref[h] = jnp.where(low, kt if h == 0 else k_sw, onehot).astype(BF16)


def _moba_flash_kernel(qa_ref, ka_ref, vt_ref, o_ref, acc_ref):
    i = pl.program_id(1)
    acc_ref[...] = jnp.zeros_like(acc_ref)
    key = lax.broadcasted_iota(jnp.int32, (ATT_TILE, ATT_TILE), 0)
    qry = lax.broadcasted_iota(jnp.int32, (ATT_TILE, ATT_TILE), 1)
    stat0 = (jnp.full((1, ATT_TILE), -jnp.inf, F32), jnp.zeros((1, ATT_TILE), F32))

    def step(j, stats, diagonal):
        off = pl.multiple_of(j * ATT_TILE, ATT_TILE)
        vt = vt_ref[j]
        out = []
        for h in range(2):
            rows = slice(h * HEAD_DIM, (h + 1) * HEAD_DIM)
            s_t = lax.dot_general(ka_ref[h, pl.ds(off, ATT_TILE), :], qa_ref[h], _CONTRACT_LANES,
                                  preferred_element_type=F32)
            if diagonal:
                s_t = jnp.where(key <= qry, s_t, -jnp.inf)
            p_t, alpha, m_new, l_new = _softmax_step(s_t, *stats[h])
            acc_ref[rows, :] = alpha * acc_ref[rows, :] + jnp.dot(vt[rows, :], p_t,
                                                                  preferred_element_type=F32)
            out.append((m_new, l_new))
        return tuple(out)

    stats = lax.fori_loop(0, i, lambda j, st: step(j, st, False), (stat0, stat0))
    stats = step(i, stats, True)
    inv = jnp.concatenate([jnp.broadcast_to(1.0 / stats[h][1], (HEAD_DIM, ATT_TILE)) for h in range(2)], axis=0)
    o_ref[...] = (acc_ref[...] * inv).T.astype(BF16)


def _moba_attention(qk, vt):
    S = qk.shape[0]
    n_pairs = MOBA_HEADS // 2
    k_col0 = D_MODEL // LANES
    aug = jax.ShapeDtypeStruct((MOBA_HEADS, S, LANES), BF16)
    qa, ka = pl.pallas_call(
        _moba_select_kernel,
        grid=(n_pairs, S // ATT_TILE),
        in_specs=[pl.BlockSpec((ATT_TILE, LANES), lambda p, i: (i, p)),
                  pl.BlockSpec((S, LANES), lambda p, i: (0, k_col0 + p))],
        out_specs=[pl.BlockSpec((2, ATT_TILE, LANES), lambda p, i: (p, i, 0))] * 2,
        out_shape=[aug, aug],
        scratch_shapes=[pltpu.VMEM((LANES, LANES), F32)],
        compiler_params=_params("parallel", "arbitrary"),
        name="moba_select",
    )(qk, qk)
    return pl.pallas_call(
        _moba_flash_kernel,
        grid=(n_pairs, S // ATT_TILE),
        in_specs=[pl.BlockSpec((2, ATT_TILE, LANES), lambda p, i: (p, i, 0)),
                  pl.BlockSpec((2, S, LANES), lambda p, i: (p, 0, 0)),
                  pl.BlockSpec((S // ATT_TILE, LANES, ATT_TILE), lambda p, i: (0, p, 0))],
        out_specs=pl.BlockSpec((ATT_TILE, LANES), lambda p, i: (i, p)),
        out_shape=jax.ShapeDtypeStruct((S, n_pairs * LANES), BF16),
        scratch_shapes=[pltpu.VMEM((LANES, ATT_TILE), F32)],
        compiler_params=_params("parallel", "parallel"),
        name="moba_flash",
    )(qa, ka, vt)


def _dilated_bias_table():
    key = np.arange(ATT_TILE)[:, None]
    qry = np.arange(ATT_TILE)[None, :]
    tiles = []
    for t in range(DILATED_KEY_TILES):
        d = qry - key + ATT_TILE * (DILATED_KEY_TILES - 1 - t)
        count = np.zeros_like(d)
        for window, dil in DILATED_PAIRS:
            count += ((d >= 0) & (d <= window) & (d % dil == 0)).astype(d.dtype)
        tiles.append(np.where(count > 0, np.log(np.maximum(count, 1)), NEG))
    return jnp.asarray(np.stack(tiles), dtype=F32)


def _dilated_kernel(q_ref, k_ref, vt_ref, tab_ref, o_ref, acc_ref):
    i = pl.program_id(1)
    acc_ref[...] = jnp.zeros_like(acc_ref)
    q = q_ref[...]
    lane = lax.broadcasted_iota(jnp.int32, q.shape, 1)
    low = lane < HEAD_DIM
    zero = jnp.zeros_like(q)
    q_heads = (jnp.where(low, q, zero), jnp.where(low, zero, q))
    last = DILATED_KEY_TILES - 1
    stat0 = (jnp.full((1, ATT_TILE), -jnp.inf, F32), jnp.zeros((1, ATT_TILE), F32))

    def body(t, stats):
        j = i - last + t
        k = k_ref[pl.ds(pl.multiple_of(j * ATT_TILE, ATT_TILE), ATT_TILE), :]
        vt = vt_ref[j]
        bias = tab_ref[t]
        out = []
        for h in range(2):
            rows = slice(h * HEAD_DIM, (h + 1) * HEAD_DIM)
            s_t = lax.dot_general(k, q_heads[h], _CONTRACT_LANES, preferred_element_type=F32) + bias
            p_t, alpha, m_new, l_new = _softmax_step(s_t, *stats[h])
            acc_ref[rows, :] = alpha * acc_ref[rows, :] + jnp.dot(vt[rows, :], p_t,
                                                                  preferred_element_type=F32)
            out.append((m_new, l_new))
        return tuple(out)

    stats = lax.fori_loop(jnp.maximum(last - i, 0), DILATED_KEY_TILES, body, (stat0, stat0))
    inv = jnp.concatenate([jnp.broadcast_to(1.0 / stats[h][1], (HEAD_DIM, ATT_TILE)) for h in range(2)], axis=0)
    o_ref[...] = (acc_ref[...] * inv).T.astype(BF16)


def _dilated_attention(qk, vt):
    S = qk.shape[0]
    n_pairs = (N_HEADS - MOBA_HEADS) // 2
    q_col0 = MOBA_HEADS // 2
    k_col0 = D_MODEL // LANES + q_col0
    return pl.pallas_call(
        _dilated_kernel,
        grid=(n_pairs, S // ATT_TILE),
        in_specs=[pl.BlockSpec((ATT_TILE, LANES), lambda p, i: (i, q_col0 + p)),
                  pl.BlockSpec((S, LANES), lambda p, i: (0, k_col0 + p)),
                  pl.BlockSpec((S // ATT_TILE, LANES, ATT_TILE), lambda p, i: (0, q_col0 + p, 0)),
                  pl.BlockSpec((DILATED_KEY_TILES, ATT_TILE, ATT_TILE), lambda p, i: (0, 0, 0))],
        out_specs=pl.BlockSpec((ATT_TILE, LANES), lambda p, i: (i, p)),
        out_shape=jax.ShapeDtypeStruct((S, n_pairs * LANES), BF16),
        scratch_shapes=[pltpu.VMEM((LANES, ATT_TILE), F32)],
        compiler_params=_params("parallel", "parallel"),
        name="dilated_attention",
    )(qk, qk, vt, _dilated_bias_table())


def _sb_kernel(q_ref, k_ref, vt_ref, o_ref, acc_ref):
    i = pl.program_id(1)
    q = q_ref[...]
    key = lax.broadcasted_iota(jnp.int32, (ATT_TILE, ATT_TILE), 0)
    qry = lax.broadcasted_iota(jnp.int32, (ATT_TILE, ATT_TILE), 1)
    suffix = jnp.where(qry >= key, 1.0, 0.0).astype(BF16)
    acc_ref[...] = jnp.zeros_like(acc_ref)

    def tile(j, off, diagonal):
        k = k_ref[pl.ds(pl.multiple_of(j * ATT_TILE, ATT_TILE), ATT_TILE), :]
        z = lax.dot_general(k, q, _CONTRACT_LANES, preferred_element_type=F32)
        sp = jnp.maximum(z, 0.0) + jnp.log(1.0 + jnp.exp(-jnp.abs(z)))
        if diagonal:
            valid = key < qry
            sp = jnp.where(valid, sp, 0.0)
        sp_hi = sp.astype(BF16)
        sp_lo = (sp - sp_hi.astype(F32)).astype(BF16)
        r_in = (jnp.dot(suffix, sp_hi, preferred_element_type=F32)
                + jnp.dot(suffix, sp_lo, preferred_element_type=F32))
        a = jnp.exp(z - (r_in + off))
        if diagonal:
            a = jnp.where(valid, a, 0.0)
        acc_ref[...] += jnp.dot(vt_ref[j], a.astype(BF16), preferred_element_type=F32)
        return off + r_in[0:1, :]

    off = tile(i, jnp.zeros((1, ATT_TILE), F32), True)

    def cond(carry):
        n, _, min_off = carry
        return jnp.logical_and(n < i, min_off <= SB_UNDERFLOW)

    def body(carry):
        n, off, _ = carry
        off = tile(i - 1 - n, off, False)
        return n + 1, off, jnp.min(off)

    lax.while_loop(cond, body, (jnp.int32(0), off, jnp.min(off)))
    o_ref[...] = acc_ref[...].T.astype(BF16)


def _stick_breaking_attention(qk, vt):
    S = qk.shape[0]
    k_col0 = D_MODEL // LANES
    return pl.pallas_call(
        _sb_kernel,
        grid=(SB_HEADS, S // ATT_TILE),
        in_specs=[pl.BlockSpec((ATT_TILE, LANES), lambda h, i: (i, h)),
                  pl.BlockSpec((S, LANES), lambda h, i: (0, k_col0 + h)),
                  pl.BlockSpec((S // ATT_TILE, LANES, ATT_TILE), lambda h, i: (0, h, 0))],
        out_specs=pl.BlockSpec((ATT_TILE, LANES), lambda h, i: (i, h)),
        out_shape=jax.ShapeDtypeStruct((S, D_MODEL), BF16),
        scratch_shapes=[pltpu.VMEM((LANES, ATT_TILE), F32)],
        compiler_params=_params("parallel", "parallel"),
        name="stick_breaking",
    )(qk, qk, vt)


def _proj_ln_kernel(*refs, n_parts):
    o_refs = refs[:n_parts]
    w_refs = refs[n_parts:2 * n_parts]
    x_ref, g_ref, b_ref, out_ref = refs[2 * n_parts:]
    m = jnp.dot(o_refs[0][...], w_refs[0][...], preferred_element_type=F32)
    for o_ref, w_ref in zip(o_refs[1:], w_refs[1:]):
        m = m + jnp.dot(o_ref[...], w_ref[...], preferred_element_type=F32)
    out_ref[...] = _layer_norm(DEEPNORM_ALPHA * x_ref[...] + m, g_ref[...], b_ref[...])


def _proj_ln(o_parts, w_parts, x, g, b):
    S = x.shape[0]
    n = len(o_parts)
    in_specs = ([pl.BlockSpec((ROW_TILE, o.shape[1]), lambda i: (i, 0)) for o in o_parts]
                + [pl.BlockSpec(w.shape, lambda i: (0, 0)) for w in w_parts]
                + [pl.BlockSpec((ROW_TILE, D_MODEL), lambda i: (i, 0)),
                   pl.BlockSpec((1, D_MODEL), lambda i: (0, 0)),
                   pl.BlockSpec((1, D_MODEL), lambda i: (0, 0))])
    return pl.pallas_call(
        functools.partial(_proj_ln_kernel, n_parts=n),
        grid=(S // ROW_TILE,),
        in_specs=in_specs,
        out_specs=pl.BlockSpec((ROW_TILE, D_MODEL), lambda i: (i, 0)),
        out_shape=jax.ShapeDtypeStruct((S, D_MODEL), F32),
        compiler_params=_params("parallel"),
        name="out_proj_ln",
    )(*o_parts, *w_parts, x, g.reshape(1, D_MODEL), b.reshape(1, D_MODEL))


def _ffn_kernel(x_ref, wg_ref, wv_ref, cwg_ref, cwv_ref, cbg_ref, cbv_ref, wd_ref, g_ref, b_ref,
                out_ref, xb_ref, acc_ref, ug_ref, uv_ref, carry_ref):
    i = pl.program_id(0)
    f = pl.program_id(1)
    rows = x_ref.shape[0]

    @pl.when(f == 0)
    def _():
        xb_ref[...] = x_ref[...].astype(BF16)
        acc_ref[...] = jnp.zeros_like(acc_ref)

    @pl.when(i == 0)
    def _():
        carry_ref[f] = jnp.zeros(carry_ref.shape[1:], F32)

    xb = xb_ref[...]

    def conv(w_ref, cw_ref, cb_ref, u_ref, part):
        u = jnp.dot(xb, w_ref[...], preferred_element_type=F32)
        u_ref[0:SUBLANES, :] = carry_ref[f, part]
        u_ref[SUBLANES:SUBLANES + rows, :] = u
        carry_ref[f, part] = u[rows - SUBLANES:rows, :]
        cw = cw_ref[...]
        c = cb_ref[...]
        for tap in range(CONV_WIDTH - 1):
            shift = CONV_WIDTH - 1 - tap
            c = c + cw[tap:tap + 1, :] * u_ref[SUBLANES - shift:SUBLANES - shift + rows, :]
        return c + cw[CONV_WIDTH - 1:CONV_WIDTH, :] * u

    cg = conv(wg_ref, cwg_ref, cbg_ref, ug_ref, 0)
    cv = conv(wv_ref, cwv_ref, cbv_ref, uv_ref, 1)
    h = cg * (1.0 / (1.0 + jnp.exp(-cg))) * cv
    acc_ref[...] += jnp.dot(h.astype(BF16), wd_ref[...], preferred_element_type=F32)

    @pl.when(f == pl.num_programs(1) - 1)
    def _():
        out_ref[...] = _layer_norm(DEEPNORM_ALPHA * x_ref[...] + acc_ref[...], g_ref[...], b_ref[...])


def _conv_ffn_ln(x, w_up, conv_w, conv_b, w_down, g, b):
    S = x.shape[0]
    n_f = FFN_DIM // FFN_COL_TILE
    tm, tf = FFN_ROW_TILE, FFN_COL_TILE
    conv_b = conv_b.reshape(1, 2 * FFN_DIM)
    return pl.pallas_call(
        _ffn_kernel,
        grid=(S // tm, n_f),
        in_specs=[pl.BlockSpec((tm, D_MODEL), lambda i, f: (i, 0)),
                  pl.BlockSpec((D_MODEL, tf), lambda i, f: (0, f)),
                  pl.BlockSpec((D_MODEL, tf), lambda i, f: (0, n_f + f)),
                  pl.BlockSpec((CONV_WIDTH, tf), lambda i, f: (0, f)),
                  pl.BlockSpec((CONV_WIDTH, tf), lambda i, f: (0, n_f + f)),
                  pl.BlockSpec((1, tf), lambda i, f: (0, f)),
                  pl.BlockSpec((1, tf), lambda i, f: (0, n_f + f)),
                  pl.BlockSpec((tf, D_MODEL), lambda i, f: (f, 0)),
                  pl.BlockSpec((1, D_MODEL), lambda i, f: (0, 0)),
                  pl.BlockSpec((1, D_MODEL), lambda i, f: (0, 0))],
        out_specs=pl.BlockSpec((tm, D_MODEL), lambda i, f: (i, 0)),
        out_shape=jax.ShapeDtypeStruct((S, D_MODEL), F32),
        scratch_shapes=[pltpu.VMEM((tm, D_MODEL), BF16),
                        pltpu.VMEM((tm, D_MODEL), F32),
                        pltpu.VMEM((tm + SUBLANES, tf), F32),
                        pltpu.VMEM((tm + SUBLANES, tf), F32),
                        pltpu.VMEM((n_f, 2, SUBLANES, tf), F32)],
        compiler_params=_params("arbitrary", "arbitrary"),
        name="conv_ffn_ln",
    )(x, w_up, w_up, conv_w, conv_w, conv_b, conv_b, w_down,
      g.reshape(1, D_MODEL), b.reshape(1, D_MODEL))


def kernel(x, w_qkv_ab, w_o_ab, w_qkv_c, w_o_c, ln_mix_g, ln_mix_b, w_up, conv_w, conv_b, w_down,
           ln_ffn_g, ln_ffn_b):
    batch, S, _ = x.shape
    assert S % ROW_TILE == 0 and S % FFN_ROW_TILE == 0 and S // MOBA_BLOCK <= HEAD_DIM
    rope_tables = _rope_tables(S)
    moba_width = MOBA_HEADS * HEAD_DIM
    outs = []
    for bi in range(batch):
        h = x[bi]
        for layer in range(DEPTH):
            if layer % 2 == 0:
                qk, vt = _qkv_proj(h, w_qkv_ab[layer // 2], HEAD_DIM ** -0.5, rope_tables)
                w_o = w_o_ab[layer // 2].astype(BF16)
                o_parts = [_moba_attention(qk, vt), _dilated_attention(qk, vt)]
                w_parts = [w_o[:moba_width], w_o[moba_width:]]
            else:
                qk, vt = _qkv_proj(h, w_qkv_c[layer // 2], SB_HEAD_DIM ** -0.5)
                o_parts = [_stick_breaking_attention(qk, vt)]
                w_parts = [w_o_c[layer // 2].astype(BF16)]
            h = _proj_ln(o_parts, w_parts, h, ln_mix_g[layer], ln_mix_b[layer])
            h = _conv_ffn_ln(h, w_up[layer].astype(BF16), conv_w[layer], conv_b[layer],
                             w_down[layer].astype(BF16), ln_ffn_g[layer], ln_ffn_b[layer])
        outs.append(h)
    return jnp.stack(outs)
```

```python
import functools

import numpy as np
import jax
import jax.numpy as jnp
from jax import lax
from jax.experimental import pallas as pl
from jax.experimental.pallas import tpu as pltpu

D_MODEL = 1024
DEPTH = 4
HEAD_DIM = 64
N_HEADS = D_MODEL // HEAD_DIM
MOBA_HEADS = N_HEADS // 4
SB_HEAD_DIM = 128
SB_HEADS = D_MODEL // SB_HEAD_DIM
MOBA_BLOCK = 256
MOBA_TOPK = 3
DILATED_PAIRS = ((128, 1), (512, 4), (2048, 16))
FFN_DIM = ((8 * D_MODEL // 3 + 127) // 128) * 128
CONV_WIDTH = 3
ROPE_THETA = 10000.0
LN_EPS = 1e-5
DEEPNORM_ALPHA = (2 * DEPTH) ** 0.25

LANES = 128
SUBLANES = 8
VMEM_LIMIT_BYTES = 56 * 1024 * 1024

ROW_TILE = 512
FFN_ROW_TILE = 1024
FFN_COL_TILE = 256
ATT_TILE = 256
MAX_DILATED_WINDOW = max(w for w, _ in DILATED_PAIRS)
DILATED_KEY_TILES = MAX_DILATED_WINDOW // ATT_TILE + 1

NEG = -1e30
SB_UNDERFLOW = 104.0
F32 = jnp.float32
BF16 = jnp.bfloat16

_CONTRACT_LANES = (((1,), (1,)), ((), ()))


def _params(*semantics, flags=None):
    return pltpu.CompilerParams(dimension_semantics=semantics, vmem_limit_bytes=VMEM_LIMIT_BYTES,
                                flags=flags)


def _layer_norm(y, g, b):
    mu = jnp.mean(y, axis=-1, keepdims=True)
    yc = y - mu
    var = jnp.mean(yc * yc, axis=-1, keepdims=True)
    return yc * lax.rsqrt(var + LN_EPS) * g + b


def _qkv_kernel(*refs, rope, q_scale):
    if rope:
        x_ref, w_ref, wvt_ref, cos_ref, sin_ref, qk_ref, vt_ref = refs
    else:
        x_ref, w_ref, wvt_ref, qk_ref, vt_ref = refs
    xb = x_ref[...].astype(BF16)
    for part in range(2):
        y = jnp.dot(xb, w_ref[:, part * D_MODEL:(part + 1) * D_MODEL], preferred_element_type=F32)
        if rope:
            c = cos_ref[...]
            s = sin_ref[...]
            lane = lax.broadcasted_iota(jnp.int32, c.shape, 1)
            first_half = (lane % HEAD_DIM) < (HEAD_DIM // 2)
            for cb in range(D_MODEL // LANES):
                yc = y[:, cb * LANES:(cb + 1) * LANES]
                partner = jnp.where(first_half,
                                    pltpu.roll(yc, LANES - HEAD_DIM // 2, 1),
                                    pltpu.roll(yc, HEAD_DIM // 2, 1))
                yc = yc * c + partner * s
                if part == 0:
                    yc = yc * q_scale
                qk_ref[:, part * D_MODEL + cb * LANES:part * D_MODEL + (cb + 1) * LANES] = yc.astype(BF16)
        else:
            if part == 0:
                y = y * q_scale
            qk_ref[:, part * D_MODEL:(part + 1) * D_MODEL] = y.astype(BF16)
    vt = lax.dot_general(wvt_ref[...], xb, _CONTRACT_LANES,
                         preferred_element_type=F32).astype(BF16)
    for t in range(ROW_TILE // ATT_TILE):
        vt_ref[t] = vt[:, t * ATT_TILE:(t + 1) * ATT_TILE]


def _qkv_proj(x, w_qkv, q_scale, rope_tables=None):
    w_qk = w_qkv[:, :2 * D_MODEL].astype(BF16)
    w_vt = w_qkv[:, 2 * D_MODEL:].T.astype(BF16)
    S = x.shape[0]
    rope = rope_tables is not None
    tiles = ROW_TILE // ATT_TILE
    in_specs = [pl.BlockSpec((ROW_TILE, D_MODEL), lambda i: (i, 0)),
                pl.BlockSpec((D_MODEL, 2 * D_MODEL), lambda i: (0, 0)),
                pl.BlockSpec((D_MODEL, D_MODEL), lambda i: (0, 0))]
    args = [x, w_qk, w_vt]
    if rope:
        in_specs += [pl.BlockSpec((ROW_TILE, LANES), lambda i: (i, 0))] * 2
        args += list(rope_tables)
    return pl.pallas_call(
        functools.partial(_qkv_kernel, rope=rope, q_scale=q_scale),
        grid=(S // ROW_TILE,),
        in_specs=in_specs,
        out_specs=[pl.BlockSpec((ROW_TILE, 2 * D_MODEL), lambda i: (i, 0)),
                   pl.BlockSpec((tiles, D_MODEL, ATT_TILE), lambda i: (i, 0, 0))],
        out_shape=[jax.ShapeDtypeStruct((S, 2 * D_MODEL), BF16),
                   jax.ShapeDtypeStruct((S // ATT_TILE, D_MODEL, ATT_TILE), BF16)],
        compiler_params=_params("parallel"),
        name="qkv_proj",
    )(*args)


def _rope_tables(S):
    inv = 1.0 / (ROPE_THETA ** (jnp.arange(0, HEAD_DIM, 2, dtype=F32) / HEAD_DIM))
    ang = jnp.arange(S, dtype=F32)[:, None] * inv[None, :]
    cos, sin = jnp.cos(ang), jnp.sin(ang)
    cos_t = jnp.tile(jnp.concatenate([cos, cos], axis=-1), (1, LANES // HEAD_DIM))
    sin_t = jnp.tile(jnp.concatenate([-sin, sin], axis=-1), (1, LANES // HEAD_DIM))
    return cos_t, sin_t


def _softmax_step(s_t, m_old, l_old):
    m_new = jnp.maximum(m_old, jnp.max(s_t, axis=0, keepdims=True))
    p_t = jnp.exp(s_t - m_new)
    alpha = jnp.exp(m_old - m_new)
    l_new = alpha * l_old + jnp.sum(p_t, axis=0, keepdims=True)
    return p_t.astype(BF16), alpha, m_new, l_new


def _attention_sweep(first, last, scores, logits, values, acc_ref, s_ref, p_ref):
    acc_ref[...] = jnp.zeros_like(acc_ref)
    p_ref[...] = jnp.zeros_like(p_ref)
    stat0 = (jnp.full((1, ATT_TILE), -jnp.inf, F32), jnp.zeros((1, ATT_TILE), F32))
    one = jnp.ones((1, ATT_TILE), F32)

    def accumulate(t, h, alpha):
        rows = slice(h * HEAD_DIM, (h + 1) * HEAD_DIM)
        acc_ref[rows, :] = alpha * acc_ref[rows, :] + jnp.dot(values(t, h), p_ref[h],
                                                              preferred_element_type=F32)

    for h in range(2):
        s_ref[h] = scores(first, h)

    def body(t, carry):
        stats, alphas = carry
        prev = jnp.maximum(t - 1, first)
        s_next = [scores(t + 1, h) for h in range(2)]
        new_stats, new_alphas = [], []
        for h in range(2):
            accumulate(prev, h, alphas[h])
            p_t, alpha, m_new, l_new = _softmax_step(logits(t, h, s_ref[h], False), *stats[h])
            p_ref[h] = p_t
            s_ref[h] = s_next[h]
            new_stats.append((m_new, l_new))
            new_alphas.append(alpha)
        return tuple(new_stats), tuple(new_alphas)

    stats, alphas = lax.fori_loop(first, last, body, ((stat0, stat0), (one, one)))
    prev = jnp.maximum(last - 1, first)
    inv = []
    for h in range(2):
        accumulate(prev, h, alphas[h])
        p_t, alpha, _, l_new = _softmax_step(logits(last, h, s_ref[h], True), *stats[h])
        p_ref[h] = p_t
        accumulate(last, h, alpha)
        inv.append(jnp.broadcast_to(1.0 / l_new, (HEAD_DIM, ATT_TILE)))
    return acc_ref[...] * jnp.concatenate(inv, axis=0)


def _moba_select_kernel(q_ref, k_ref, qa_ref, ka_ref, km_ref):
    i = pl.program_id(1)
    n_blocks = k_ref.shape[0] // MOBA_BLOCK

    @pl.when(i == 0)
    def _():
        km_ref[...] = jnp.zeros_like(km_ref)

        def body(b, carry):
            kb = k_ref[pl.ds(pl.multiple_of(b * MOBA_BLOCK, MOBA_BLOCK), MOBA_BLOCK), :].astype(F32)
            km_ref[pl.ds(b, 1), :] = jnp.sum(kb, axis=0, keepdims=True) * (1.0 / MOBA_BLOCK)
            return carry

        lax.fori_loop(0, n_blocks, body, 0)

    q = q_ref[...].astype(F32)
    kt = k_ref[pl.ds(pl.multiple_of(i * ATT_TILE, ATT_TILE), ATT_TILE), :].astype(F32)
    lane_i = lax.broadcasted_iota(jnp.int32, q.shape, 1)
    lane = lane_i.astype(F32)
    low = lane_i < HEAD_DIM
    blk = i.astype(F32)
    km = km_ref[...]
    onehot = jnp.where(lane_i - HEAD_DIM == i, 1.0, 0.0)
    q_sw = pltpu.roll(q, HEAD_DIM, 1)
    k_sw = pltpu.roll(kt, HEAD_DIM, 1)
    for h in range(2):
        qh = jnp.where(low if h == 0 else jnp.logical_not(low), q, 0.0)
        gate = lax.dot_general(qh, km, _CONTRACT_LANES, precision=lax.Precision.HIGHEST,
                               preferred_element_type=F32)
        g = jnp.where(lane < blk, gate, -jnp.inf)
        sel = lane == blk
        for _ in range(MOBA_TOPK):
            m = jnp.max(g, axis=1, keepdims=True)
            first = jnp.min(jnp.where(g == m, lane, float(LANES)), axis=1, keepdims=True)
            pick = jnp.logical_and(lane == first, m > -jnp.inf)
            sel = jnp.logical_or(sel, pick)
            g = jnp.where(pick, -jnp.inf, g)
        bias = pltpu.roll(jnp.where(sel, 0.0, NEG), HEAD_DIM, 1)
        qa_ref[h] = jnp.where(low, q if h == 0 else q_sw, bias).astype(BF16)
        ka_ref[h] = jnp.where(low, kt if h == 0 else k_sw, onehot).astype(BF16)


def _moba_flash_kernel(qa_ref, ka_ref, vt_ref, o_ref, acc_ref, s_ref, p_ref):
    i = pl.program_id(1)
    key = lax.broadcasted_iota(jnp.int32, (ATT_TILE, ATT_TILE), 0)
    qry = lax.broadcasted_iota(jnp.int32, (ATT_TILE, ATT_TILE), 1)

    def scores(j, h):
        off = pl.multiple_of(j * ATT_TILE, ATT_TILE)
        return lax.dot_general(ka_ref[h, pl.ds(off, ATT_TILE), :], qa_ref[h], _CONTRACT_LANES,
                               preferred_element_type=F32)

    def logits(j, h, s_t, is_last):
        return jnp.where(key <= qry, s_t, -jnp.inf) if is_last else s_t

    def values(j, h):
        return vt_ref[j, h * HEAD_DIM:(h + 1) * HEAD_DIM, :]

    o_t = _attention_sweep(0, i, scores, logits, values, acc_ref, s_ref, p_ref)
    o_ref[...] = o_t.T.astype(BF16)


def _moba_attention(qk, vt):
    S = qk.shape[0]
    n_pairs = MOBA_HEADS // 2
    k_col0 = D_MODEL // LANES
    aug = jax.ShapeDtypeStruct((MOBA_HEADS, S, LANES), BF16)
    qa, ka = pl.pallas_call(
        _moba_select_kernel,
        grid=(n_pairs, S // ATT_TILE),
        in_specs=[pl.BlockSpec((ATT_TILE, LANES), lambda p, i: (i, p)),
                  pl.BlockSpec((S, LANES), lambda p, i: (0, k_col0 + p))],
        out_specs=[pl.BlockSpec((2, ATT_TILE, LANES), lambda p, i: (p, i, 0))] * 2,
        out_shape=[aug, aug],
        scratch_shapes=[pltpu.VMEM((LANES, LANES), F32)],
        compiler_params=_params("parallel", "arbitrary"),
        name="moba_select",
    )(qk, qk)
    return pl.pallas_call(
        _moba_flash_kernel,
        grid=(n_pairs, S // ATT_TILE),
        in_specs=[pl.BlockSpec((2, ATT_TILE, LANES), lambda p, i: (p, i, 0)),
                  pl.BlockSpec((2, S, LANES), lambda p, i: (p, 0, 0)),
                  pl.BlockSpec((S // ATT_TILE, LANES, ATT_TILE), lambda p, i: (0, p, 0))],
        out_specs=pl.BlockSpec((ATT_TILE, LANES), lambda p, i: (i, p)),
        out_shape=jax.ShapeDtypeStruct((S, n_pairs * LANES), BF16),
        scratch_shapes=[pltpu.VMEM((LANES, ATT_TILE), F32),
                        pltpu.VMEM((2, ATT_TILE, ATT_TILE), F32),
                        pltpu.VMEM((2, ATT_TILE, ATT_TILE), BF16)],
        compiler_params=_params("parallel", "parallel"),
        name="moba_flash",
    )(qa, ka, vt)


def _dilated_bias_table():
    key = np.arange(ATT_TILE)[:, None]
    qry = np.arange(ATT_TILE)[None, :]
    tiles = []
    for t in range(DILATED_KEY_TILES):
        d = qry - key + ATT_TILE * (DILATED_KEY_TILES - 1 - t)
        count = np.zeros_like(d)
        for window, dil in DILATED_PAIRS:
            count += ((d >= 0) & (d <= window) & (d % dil == 0)).astype(d.dtype)
        tiles.append(np.where(count > 0, np.log(np.maximum(count, 1)), NEG))
    return jnp.asarray(np.stack(tiles), dtype=F32)


def _dilated_kernel(q_ref, k_ref, vt_ref, tab_ref, o_ref, acc_ref, s_ref, p_ref, qh_ref):
    i = pl.program_id(1)
    q = q_ref[...]
    low = lax.broadcasted_iota(jnp.int32, q.shape, 1) < HEAD_DIM
    zero = jnp.zeros_like(q)
    qh_ref[0] = jnp.where(low, q, zero)
    qh_ref[1] = jnp.where(low, zero, q)
    last = DILATED_KEY_TILES - 1

    def scores(t, h):
        off = pl.multiple_of((i - last + t) * ATT_TILE, ATT_TILE)
        return lax.dot_general(k_ref[pl.ds(off, ATT_TILE), :], qh_ref[h], _CONTRACT_LANES,
                               preferred_element_type=F32)

    def logits(t, h, s_t, is_last):
        return s_t + tab_ref[t]

    def values(t, h):
        return vt_ref[i - last + t, h * HEAD_DIM:(h + 1) * HEAD_DIM, :]

    o_t = _attention_sweep(jnp.maximum(last - i, 0), last, scores, logits, values, acc_ref, s_ref, p_ref)
    o_ref[...] = o_t.T.astype(BF16)


def _dilated_attention(qk, vt):
    S = qk.shape[0]
    n_pairs = (N_HEADS - MOBA_HEADS) // 2
    q_col0 = MOBA_HEADS // 2
    k_col0 = D_MODEL // LANES + q_col0
    return pl.pallas_call(
        _dilated_kernel,
        grid=(n_pairs, S // ATT_TILE),
        in_specs=[pl.BlockSpec((ATT_TILE, LANES), lambda p, i: (i, q_col0 + p)),
                  pl.BlockSpec((S, LANES), lambda p, i: (0, k_col0 + p)),
                  pl.BlockSpec((S // ATT_TILE, LANES, ATT_TILE), lambda p, i: (0, q_col0 + p, 0)),
                  pl.BlockSpec((DILATED_KEY_TILES, ATT_TILE, ATT_TILE), lambda p, i: (0, 0, 0))],
        out_specs=pl.BlockSpec((ATT_TILE, LANES), lambda p, i: (i, p)),
        out_shape=jax.ShapeDtypeStruct((S, n_pairs * LANES), BF16),
        scratch_shapes=[pltpu.VMEM((LANES, ATT_TILE), F32),
                        pltpu.VMEM((2, ATT_TILE, ATT_TILE), F32),
                        pltpu.VMEM((2, ATT_TILE, ATT_TILE), BF16),
                        pltpu.VMEM((2, ATT_TILE, LANES), BF16)],
        compiler_params=_params("parallel", "parallel"),
        name="dilated_attention",
    )(qk, qk, vt, _dilated_bias_table())


def _sb_kernel(q_ref, k_ref, vt_ref, o_ref, acc_ref):
    i = pl.program_id(1)
    q = q_ref[...]
    key = lax.broadcasted_iota(jnp.int32, (ATT_TILE, ATT_TILE), 0)
    qry = lax.broadcasted_iota(jnp.int32, (ATT_TILE, ATT_TILE), 1)
    suffix = jnp.where(qry >= key, 1.0, 0.0).astype(BF16)
    acc_ref[...] = jnp.zeros_like(acc_ref)

    def tile(j, off, diagonal):
        k = k_ref[pl.ds(pl.multiple_of(j * ATT_TILE, ATT_TILE), ATT_TILE), :]
        z = lax.dot_general(k, q, _CONTRACT_LANES, preferred_element_type=F32)
        sp = jnp.maximum(z, 0.0) + jnp.log(1.0 + jnp.exp(-jnp.abs(z)))
        if diagonal:
            valid = key < qry
            sp = jnp.where(valid, sp, 0.0)
        sp_hi = sp.astype(BF16)
        sp_lo = (sp - sp_hi.astype(F32)).astype(BF16)
        r_in = (jnp.dot(suffix, sp_hi, preferred_element_type=F32)
                + jnp.dot(suffix, sp_lo, preferred_element_type=F32))
        a = jnp.exp(z - (r_in + off))
        if diagonal:
            a = jnp.where(valid, a, 0.0)
        acc_ref[...] += jnp.dot(vt_ref[j], a.astype(BF16), preferred_element_type=F32)
        return off + r_in[0:1, :]

    off = tile(i, jnp.zeros((1, ATT_TILE), F32), True)

    def cond(carry):
        n, _, min_off = carry
        return jnp.logical_and(n < i, min_off <= SB_UNDERFLOW)

    def body(carry):
        n, off, _ = carry
        off = tile(i - 1 - n, off, False)
        return n + 1, off, jnp.min(off)

    lax.while_loop(cond, body, (jnp.int32(0), off, jnp.min(off)))
    o_ref[...] = acc_ref[...].T.astype(BF16)


def _stick_breaking_attention(qk, vt):
    S = qk.shape[0]
    k_col0 = D_MODEL // LANES
    return pl.pallas_call(
        _sb_kernel,
        grid=(SB_HEADS, S // ATT_TILE),
        in_specs=[pl.BlockSpec((ATT_TILE, LANES), lambda h, i: (i, h)),
                  pl.BlockSpec((S, LANES), lambda h, i: (0, k_col0 + h)),
                  pl.BlockSpec((S // ATT_TILE, LANES, ATT_TILE), lambda h, i: (0, h, 0))],
        out_specs=pl.BlockSpec((ATT_TILE, LANES), lambda h, i: (i, h)),
        out_shape=jax.ShapeDtypeStruct((S, D_MODEL), BF16),
        scratch_shapes=[pltpu.VMEM((LANES, ATT_TILE), F32)],
        compiler_params=_params("parallel", "parallel"),
        name="stick_breaking",
    )(qk, qk, vt)


def _proj_ln_kernel(*refs, n_parts):
    o_refs = refs[:n_parts]
    w_refs = refs[n_parts:2 * n_parts]
    x_ref, g_ref, b_ref, out_ref = refs[2 * n_parts:]
    m = jnp.dot(o_refs[0][...], w_refs[0][...], preferred_element_type=F32)
    for o_ref, w_ref in zip(o_refs[1:], w_refs[1:]):
        m = m + jnp.dot(o_ref[...], w_ref[...], preferred_element_type=F32)
    out_ref[...] = _layer_norm(DEEPNORM_ALPHA * x_ref[...] + m, g_ref[...], b_ref[...])


def _proj_ln(o_parts, w_parts, x, g, b):
    S = x.shape[0]
    n = len(o_parts)
    in_specs = ([pl.BlockSpec((ROW_TILE, o.shape[1]), lambda i: (i, 0)) for o in o_parts]
                + [pl.BlockSpec(w.shape, lambda i: (0, 0)) for w in w_parts]
                + [pl.BlockSpec((ROW_TILE, D_MODEL), lambda i: (i, 0)),
                   pl.BlockSpec((1, D_MODEL), lambda i: (0, 0)),
                   pl.BlockSpec((1, D_MODEL), lambda i: (0, 0))])
    return pl.pallas_call(
        functools.partial(_proj_ln_kernel, n_parts=n),
        grid=(S // ROW_TILE,),
        in_specs=in_specs,
        out_specs=pl.BlockSpec((ROW_TILE, D_MODEL), lambda i: (i, 0)),
        out_shape=jax.ShapeDtypeStruct((S, D_MODEL), F32),
        compiler_params=_params("parallel"),
        name="out_proj_ln",
    )(*o_parts, *w_parts, x, g.reshape(1, D_MODEL), b.reshape(1, D_MODEL))


def _ffn_kernel(x_ref, wg_ref, wv_ref, cwg_ref, cwv_ref, cbg_ref, cbv_ref, wd_ref, g_ref, b_ref,
                out_ref, xb_ref, acc_ref, ug_ref, uv_ref, carry_ref):
    i = pl.program_id(0)
    f = pl.program_id(1)
    rows = x_ref.shape[0]

    @pl.when(f == 0)
    def _():
        xb_ref[...] = x_ref[...].astype(BF16)
        acc_ref[...] = jnp.zeros_like(acc_ref)

    @pl.when(i == 0)
    def _():
        carry_ref[f] = jnp.zeros(carry_ref.shape[1:], F32)

    xb = xb_ref[...]

    def conv(w_ref, cw_ref, cb_ref, u_ref, part):
        u = jnp.dot(xb, w_ref[...], preferred_element_type=F32)
        u_ref[0:SUBLANES, :] = carry_ref[f, part]
        u_ref[SUBLANES:SUBLANES + rows, :] = u
        carry_ref[f, part] = u[rows - SUBLANES:rows, :]
        cw = cw_ref[...]
        c = cb_ref[...]
        for tap in range(CONV_WIDTH - 1):
            shift = CONV_WIDTH - 1 - tap
            c = c + cw[tap:tap + 1, :] * u_ref[SUBLANES - shift:SUBLANES - shift + rows, :]
        return c + cw[CONV_WIDTH - 1:CONV_WIDTH, :] * u

    cg = conv(wg_ref, cwg_ref, cbg_ref, ug_ref, 0)
    cv = conv(wv_ref, cwv_ref, cbv_ref, uv_ref, 1)
    h = cg * (1.0 / (1.0 + jnp.exp(-cg))) * cv
    acc_ref[...] += jnp.dot(h.astype(BF16), wd_ref[...], preferred_element_type=F32)

    @pl.when(f == pl.num_programs(1) - 1)
    def _():
        out_ref[...] = _layer_norm(DEEPNORM_ALPHA * x_ref[...] + acc_ref[...], g_ref[...], b_ref[...])


def _conv_ffn_ln(x, w_up, conv_w, conv_b, w_down, g, b):
    S = x.shape[0]
    n_f = FFN_DIM // FFN_COL_TILE
    tm, tf = FFN_ROW_TILE, FFN_COL_TILE
    conv_b = conv_b.reshape(1, 2 * FFN_DIM)
    return pl.pallas_call(
        _ffn_kernel,
        grid=(S // tm, n_f),
        in_specs=[pl.BlockSpec((tm, D_MODEL), lambda i, f: (i, 0)),
                  pl.BlockSpec((D_MODEL, tf), lambda i, f: (0, f)),
                  pl.BlockSpec((D_MODEL, tf), lambda i, f: (0, n_f + f)),
                  pl.BlockSpec((CONV_WIDTH, tf), lambda i, f: (0, f)),
                  pl.BlockSpec((CONV_WIDTH, tf), lambda i, f: (0, n_f + f)),
                  pl.BlockSpec((1, tf), lambda i, f: (0, f)),
                  pl.BlockSpec((1, tf), lambda i, f: (0, n_f + f)),
                  pl.BlockSpec((tf, D_MODEL), lambda i, f: (f, 0)),
                  pl.BlockSpec((1, D_MODEL), lambda i, f: (0, 0)),
                  pl.BlockSpec((1, D_MODEL), lambda i, f: (0, 0))],
        out_specs=pl.BlockSpec((tm, D_MODEL), lambda i, f: (i, 0)),
        out_shape=jax.ShapeDtypeStruct((S, D_MODEL), F32),
        scratch_shapes=[pltpu.VMEM((tm, D_MODEL), BF16),
                        pltpu.VMEM((tm, D_MODEL), F32),
                        pltpu.VMEM((tm + SUBLANES, tf), F32),
                        pltpu.VMEM((tm + SUBLANES, tf), F32),
                        pltpu.VMEM((n_f, 2, SUBLANES, tf), F32)],
        compiler_params=_params("arbitrary", "arbitrary"),
        name="conv_ffn_ln",
    )(x, w_up, w_up, conv_w, conv_w, conv_b, conv_b, w_down,
      g.reshape(1, D_MODEL), b.reshape(1, D_MODEL))


def kernel(x, w_qkv_ab, w_o_ab, w_qkv_c, w_o_c, ln_mix_g, ln_mix_b, w_up, conv_w, conv_b, w_down,
           ln_ffn_g, ln_ffn_b):
    batch, S, _ = x.shape
    assert S % ROW_TILE == 0 and S % FFN_ROW_TILE == 0 and S // MOBA_BLOCK <= HEAD_DIM
    rope_tables = _rope_tables(S)
    moba_width = MOBA_HEADS * HEAD_DIM
    outs = []
    for bi in range(batch):
        h = x[bi]
        for layer in range(DEPTH):
            if layer % 2 == 0:
                qk, vt = _qkv_proj(h, w_qkv_ab[layer // 2], HEAD_DIM ** -0.5, rope_tables)
                w_o = w_o_ab[layer // 2].astype(BF16)
                o_parts = [_moba_attention(qk, vt), _dilated_attention(qk, vt)]
                w_parts = [w_o[:moba_width], w_o[moba_width:]]
            else:
                qk, vt = _qkv_proj(h, w_qkv_c[layer // 2], SB_HEAD_DIM ** -0.5)
                o_parts = [_stick_breaking_attention(qk, vt)]
                w_parts = [w_o_c[layer // 2].astype(BF16)]
            h = _proj_ln(o_parts, w_parts, h, ln_mix_g[layer], ln_mix_b[layer])
            h = _conv_ffn_ln(h, w_up[layer].astype(BF16), conv_w[layer], conv_b[layer],
                             w_down[layer].astype(BF16), ln_ffn_g[layer], ln_ffn_b[layer])
        outs.append(h)
    return jnp.stack(outs)
```

```python
import functools

import numpy as np
import jax
import jax.numpy as jnp
from jax import lax
from jax.experimental import pallas as pl
from jax.experimental.pallas import tpu as pltpu

D_MODEL = 1024
DEPTH = 4
HEAD_DIM = 64
N_HEADS = D_MODEL // HEAD_DIM
MOBA_HEADS = N_HEADS // 4
SB_HEAD_DIM = 128
SB_HEADS = D_MODEL // SB_HEAD_DIM
MOBA_BLOCK = 256
MOBA_TOPK = 3
DILATED_PAIRS = ((128, 1), (512, 4), (2048, 16))
FFN_DIM = ((8 * D_MODEL // 3 + 127) // 128) * 128
CONV_WIDTH = 3
ROPE_THETA = 10000.0
LN_EPS = 1e-5
DEEPNORM_ALPHA = (2 * DEPTH) ** 0.25

LANES = 128
SUBLANES = 8
VMEM_LIMIT_BYTES = 56 * 1024 * 1024

ROW_TILE = 512
FFN_ROW_TILE = 512
FFN_COL_TILE = 256
ATT_TILE = 256
MAX_DILATED_WINDOW = max(w for w, _ in DILATED_PAIRS)
DILATED_KEY_TILES = MAX_DILATED_WINDOW // ATT_TILE + 1

NEG = -1e30
SB_UNDERFLOW = 104.0
F32 = jnp.float32
BF16 = jnp.bfloat16

_CONTRACT_LANES = (((1,), (1,)), ((), ()))


def _params(*semantics, flags=None):
    return pltpu.CompilerParams(dimension_semantics=semantics, vmem_limit_bytes=VMEM_LIMIT_BYTES,
                                flags=flags)


def _layer_norm(y, g, b):
    mu = jnp.mean(y, axis=-1, keepdims=True)
    yc = y - mu
    var = jnp.mean(yc * yc, axis=-1, keepdims=True)
    return yc * lax.rsqrt(var + LN_EPS) * g + b


def _qkv_kernel(*refs, rope, q_scale):
    if rope:
        x_ref, w_ref, wvt_ref, cos_ref, sin_ref, qk_ref, vt_ref = refs
    else:
        x_ref, w_ref, wvt_ref, qk_ref, vt_ref = refs
    xb = x_ref[...].astype(BF16)
    for part in range(2):
        y = jnp.dot(xb, w_ref[:, part * D_MODEL:(part + 1) * D_MODEL], preferred_element_type=F32)
        if rope:
            c = cos_ref[...]
            s = sin_ref[...]
            lane = lax.broadcasted_iota(jnp.int32, c.shape, 1)
            first_half = (lane % HEAD_DIM) < (HEAD_DIM // 2)
            for cb in range(D_MODEL // LANES):
                yc = y[:, cb * LANES:(cb + 1) * LANES]
                partner = jnp.where(first_half,
                                    pltpu.roll(yc, LANES - HEAD_DIM // 2, 1),
                                    pltpu.roll(yc, HEAD_DIM // 2, 1))
                yc = yc * c + partner * s
                if part == 0:
                    yc = yc * q_scale
                qk_ref[:, part * D_MODEL + cb * LANES:part * D_MODEL + (cb + 1) * LANES] = yc.astype(BF16)
        else:
            if part == 0:
                y = y * q_scale
            qk_ref[:, part * D_MODEL:(part + 1) * D_MODEL] = y.astype(BF16)
    vt = lax.dot_general(wvt_ref[...], xb, _CONTRACT_LANES,
                         preferred_element_type=F32).astype(BF16)
    for t in range(ROW_TILE // ATT_TILE):
        vt_ref[t] = vt[:, t * ATT_TILE:(t + 1) * ATT_TILE]


def _qkv_proj(x, w_qkv, q_scale, rope_tables=None):
    w_qk = w_qkv[:, :2 * D_MODEL].astype(BF16)
    w_vt = w_qkv[:, 2 * D_MODEL:].T.astype(BF16)
    S = x.shape[0]
    rope = rope_tables is not None
    tiles = ROW_TILE // ATT_TILE
    in_specs = [pl.BlockSpec((ROW_TILE, D_MODEL), lambda i: (i, 0)),
                pl.BlockSpec((D_MODEL, 2 * D_MODEL), lambda i: (0, 0)),
                pl.BlockSpec((D_MODEL, D_MODEL), lambda i: (0, 0))]
    args = [x, w_qk, w_vt]
    if rope:
        in_specs += [pl.BlockSpec((ROW_TILE, LANES), lambda i: (i, 0))] * 2
        args += list(rope_tables)
    return pl.pallas_call(
        functools.partial(_qkv_kernel, rope=rope, q_scale=q_scale),
        grid=(S // ROW_TILE,),
        in_specs=in_specs,
        out_specs=[pl.BlockSpec((ROW_TILE, 2 * D_MODEL), lambda i: (i, 0)),
                   pl.BlockSpec((tiles, D_MODEL, ATT_TILE), lambda i: (i, 0, 0))],
        out_shape=[jax.ShapeDtypeStruct((S, 2 * D_MODEL), BF16),
                   jax.ShapeDtypeStruct((S // ATT_TILE, D_MODEL, ATT_TILE), BF16)],
        compiler_params=_params("parallel"),
        name="qkv_proj",
    )(*args)


def _rope_tables(S):
    inv = 1.0 / (ROPE_THETA ** (jnp.arange(0, HEAD_DIM, 2, dtype=F32) / HEAD_DIM))
    ang = jnp.arange(S, dtype=F32)[:, None] * inv[None, :]
    cos, sin = jnp.cos(ang), jnp.sin(ang)
    cos_t = jnp.tile(jnp.concatenate([cos, cos], axis=-1), (1, LANES // HEAD_DIM))
    sin_t = jnp.tile(jnp.concatenate([-sin, sin], axis=-1), (1, LANES // HEAD_DIM))
    return cos_t, sin_t


def _softmax_step(s_t, m_old, l_old):
    m_new = jnp.maximum(m_old, jnp.max(s_t, axis=0, keepdims=True))
    p_t = jnp.exp(s_t - m_new)
    alpha = jnp.exp(m_old - m_new)
    l_new = alpha * l_old + jnp.sum(p_t, axis=0, keepdims=True)
    return p_t.astype(BF16), alpha, m_new, l_new


def _attention_sweep(first, last, scores, logits, values, acc_ref, s_ref, p_ref):
    acc_ref[...] = jnp.zeros_like(acc_ref)
    p_ref[...] = jnp.zeros_like(p_ref)
    stat0 = (jnp.full((1, ATT_TILE), -jnp.inf, F32), jnp.zeros((1, ATT_TILE), F32))
    one = jnp.ones((1, ATT_TILE), F32)

    def accumulate(t, h, alpha):
        rows = slice(h * HEAD_DIM, (h + 1) * HEAD_DIM)
        acc_ref[rows, :] = alpha * acc_ref[rows, :] + jnp.dot(values(t, h), p_ref[h],
                                                              preferred_element_type=F32)

    for h in range(2):
        s_ref[h] = scores(first, h)

    def body(t, carry):
        stats, alphas = carry
        prev = jnp.maximum(t - 1, first)
        s_next = [scores(t + 1, h) for h in range(2)]
        new_stats, new_alphas = [], []
        for h in range(2):
            accumulate(prev, h, alphas[h])
            p_t, alpha, m_new, l_new = _softmax_step(logits(t, h, s_ref[h], False), *stats[h])
            p_ref[h] = p_t
            s_ref[h] = s_next[h]
            new_stats.append((m_new, l_new))
            new_alphas.append(alpha)
        return tuple(new_stats), tuple(new_alphas)

    stats, alphas = lax.fori_loop(first, last, body, ((stat0, stat0), (one, one)))
    prev = jnp.maximum(last - 1, first)
    inv = []
    for h in range(2):
        accumulate(prev, h, alphas[h])
        p_t, alpha, _, l_new = _softmax_step(logits(last, h, s_ref[h], True), *stats[h])
        p_ref[h] = p_t
        accumulate(last, h, alpha)
        inv.append(jnp.broadcast_to(1.0 / l_new, (HEAD_DIM, ATT_TILE)))
    return acc_ref[...] * jnp.concatenate(inv, axis=0)


def _moba_select_kernel(q_ref, k_ref, qa_ref, ka_ref, km_ref):
    i = pl.program_id(1)
    n_blocks = k_ref.shape[0] // MOBA_BLOCK

    @pl.when(i == 0)
    def _():
        km_ref[...] = jnp.zeros_like(km_ref)

        def body(b, carry):
            kb = k_ref[pl.ds(pl.multiple_of(b * MOBA_BLOCK, MOBA_BLOCK), MOBA_BLOCK), :].astype(F32)
            km_ref[pl.ds(b, 1), :] = jnp.sum(kb, axis=0, keepdims=True) * (1.0 / MOBA_BLOCK)
            return carry

        lax.fori_loop(0, n_blocks, body, 0)

    q = q_ref[...].astype(F32)
    kt = k_ref[pl.ds(pl.multiple_of(i * ATT_TILE, ATT_TILE), ATT_TILE), :].astype(F32)
    lane_i = lax.broadcasted_iota(jnp.int32, q.shape, 1)
    lane = lane_i.astype(F32)
    low = lane_i < HEAD_DIM
    blk = i.astype(F32)
    km = km_ref[...]
    onehot = jnp.where(lane_i - HEAD_DIM == i, 1.0, 0.0)
    q_sw = pltpu.roll(q, HEAD_DIM, 1)
    k_sw = pltpu.roll(kt, HEAD_DIM, 1)
    for h in range(2):
        qh = jnp.where(low if h == 0 else jnp.logical_not(low), q, 0.0)
        gate = lax.dot_general(qh, km, _CONTRACT_LANES, precision=lax.Precision.HIGHEST,
                               preferred_element_type=F32)
        g = jnp.where(lane < blk, gate, -jnp.inf)
        sel = lane == blk
        for _ in range(MOBA_TOPK):
            m = jnp.max(g, axis=1, keepdims=True)
            first = jnp.min(jnp.where(g == m, lane, float(LANES)), axis=1, keepdims=True)
            pick = jnp.logical_and(lane == first, m > -jnp.inf)
            sel = jnp.logical_or(sel, pick)
            g = jnp.where(pick, -jnp.inf, g)
        bias = pltpu.roll(jnp.where(sel, 0.0, NEG), HEAD_DIM, 1)
        qa_ref[h] = jnp.where(low, q if h == 0 else q_sw, bias).astype(BF16)
        ka_ref[h] = jnp.where(low, kt if h == 0 else k_sw, onehot).astype(BF16)


def _moba_flash_kernel(qa_ref, ka_ref, vt_ref, o_ref, acc_ref, s_ref, p_ref):
    i = pl.program_id(1)
    key = lax.broadcasted_iota(jnp.int32, (ATT_TILE, ATT_TILE), 0)
    qry = lax.broadcasted_iota(jnp.int32, (ATT_TILE, ATT_TILE), 1)

    def scores(j, h):
        off = pl.multiple_of(j * ATT_TILE, ATT_TILE)
        return lax.dot_general(ka_ref[h, pl.ds(off, ATT_TILE), :], qa_ref[h], _CONTRACT_LANES,
                               preferred_element_type=F32)

    def logits(j, h, s_t, is_last):
        return jnp.where(key <= qry, s_t, -jnp.inf) if is_last else s_t

    def values(j, h):
        return vt_ref[j, h * HEAD_DIM:(h + 1) * HEAD_DIM, :]

    o_t = _attention_sweep(0, i, scores, logits, values, acc_ref, s_ref, p_ref)
    o_ref[...] = o_t.T.astype(BF16)


def _moba_attention(qk, vt):
    S = qk.shape[0]
    n_pairs = MOBA_HEADS // 2
    k_col0 = D_MODEL // LANES
    aug = jax.ShapeDtypeStruct((MOBA_HEADS, S, LANES), BF16)
    qa, ka = pl.pallas_call(
        _moba_select_kernel,
        grid=(n_pairs, S // ATT_TILE),
        in_specs=[pl.BlockSpec((ATT_TILE, LANES), lambda p, i: (i, p)),
                  pl.BlockSpec((S, LANES), lambda p, i: (0, k_col0 + p))],
        out_specs=[pl.BlockSpec((2, ATT_TILE, LANES), lambda p, i: (p, i, 0))] * 2,
        out_shape=[aug, aug],
        scratch_shapes=[pltpu.VMEM((LANES, LANES), F32)],
        compiler_params=_params("parallel", "arbitrary"),
        name="moba_select",
    )(qk, qk)
    return pl.pallas_call(
        _moba_flash_kernel,
        grid=(n_pairs, S // ATT_TILE),
        in_specs=[pl.BlockSpec((2, ATT_TILE, LANES), lambda p, i: (p, i, 0)),
                  pl.BlockSpec((2, S, LANES), lambda p, i: (p, 0, 0)),
                  pl.BlockSpec((S // ATT_TILE, LANES, ATT_TILE), lambda p, i: (0, p, 0))],
        out_specs=pl.BlockSpec((ATT_TILE, LANES), lambda p, i: (i, p)),
        out_shape=jax.ShapeDtypeStruct((S, n_pairs * LANES), BF16),
        scratch_shapes=[pltpu.VMEM((LANES, ATT_TILE), F32),
                        pltpu.VMEM((2, ATT_TILE, ATT_TILE), F32),
                        pltpu.VMEM((2, ATT_TILE, ATT_TILE), BF16)],
        compiler_params=_params("parallel", "parallel"),
        name="moba_flash",
    )(qa, ka, vt)


def _dilated_bias_table():
    key = np.arange(ATT_TILE)[:, None]
    qry = np.arange(ATT_TILE)[None, :]
    tiles = []
    for t in range(DILATED_KEY_TILES):
        d = qry - key + ATT_TILE * (DILATED_KEY_TILES - 1 - t)
        count = np.zeros_like(d)
        for window, dil in DILATED_PAIRS:
            count += ((d >= 0) & (d <= window) & (d % dil == 0)).astype(d.dtype)
        tiles.append(np.where(count > 0, np.log(np.maximum(count, 1)), NEG))
    return jnp.asarray(np.stack(tiles), dtype=F32)


def _dilated_kernel(q_ref, k_ref, vt_ref, tab_ref, o_ref, acc_ref, s_ref, p_ref, qh_ref):
    i = pl.program_id(1)
    q = q_ref[...]
    low = lax.broadcasted_iota(jnp.int32, q.shape, 1) < HEAD_DIM
    zero = jnp.zeros_like(q)
    qh_ref[0] = jnp.where(low, q, zero)
    qh_ref[1] = jnp.where(low, zero, q)
    last = DILATED_KEY_TILES - 1

    def scores(t, h):
        off = pl.multiple_of((i - last + t) * ATT_TILE, ATT_TILE)
        return lax.dot_general(k_ref[pl.ds(off, ATT_TILE), :], qh_ref[h], _CONTRACT_LANES,
                               preferred_element_type=F32)

    def logits(t, h, s_t, is_last):
        return s_t + tab_ref[t]

    def values(t, h):
        return vt_ref[i - last + t, h * HEAD_DIM:(h + 1) * HEAD_DIM, :]

    o_t = _attention_sweep(jnp.maximum(last - i, 0), last, scores, logits, values, acc_ref, s_ref, p_ref)
    o_ref[...] = o_t.T.astype(BF16)


def _dilated_attention(qk, vt):
    S = qk.shape[0]
    n_pairs = (N_HEADS - MOBA_HEADS) // 2
    q_col0 = MOBA_HEADS // 2
    k_col0 = D_MODEL // LANES + q_col0
    return pl.pallas_call(
        _dilated_kernel,
        grid=(n_pairs, S // ATT_TILE),
        in_specs=[pl.BlockSpec((ATT_TILE, LANES), lambda p, i: (i, q_col0 + p)),
                  pl.BlockSpec((S, LANES), lambda p, i: (0, k_col0 + p)),
                  pl.BlockSpec((S // ATT_TILE, LANES, ATT_TILE), lambda p, i: (0, q_col0 + p, 0)),
                  pl.BlockSpec((DILATED_KEY_TILES, ATT_TILE, ATT_TILE), lambda p, i: (0, 0, 0))],
        out_specs=pl.BlockSpec((ATT_TILE, LANES), lambda p, i: (i, p)),
        out_shape=jax.ShapeDtypeStruct((S, n_pairs * LANES), BF16),
        scratch_shapes=[pltpu.VMEM((LANES, ATT_TILE), F32),
                        pltpu.VMEM((2, ATT_TILE, ATT_TILE), F32),
                        pltpu.VMEM((2, ATT_TILE, ATT_TILE), BF16),
                        pltpu.VMEM((2, ATT_TILE, LANES), BF16)],
        compiler_params=_params("parallel", "parallel"),
        name="dilated_attention",
    )(qk, qk, vt, _dilated_bias_table())


def _sb_kernel(q_ref, k_ref, vt_ref, o_ref, acc_ref):
    i = pl.program_id(1)
    q = q_ref[...]
    key = lax.broadcasted_iota(jnp.int32, (ATT_TILE, ATT_TILE), 0)
    qry = lax.broadcasted_iota(jnp.int32, (ATT_TILE, ATT_TILE), 1)
    suffix = jnp.where(qry >= key, 1.0, 0.0).astype(BF16)
    acc_ref[...] = jnp.zeros_like(acc_ref)

    valid = key < qry

    def front(j, diagonal):
        k = k_ref[pl.ds(pl.multiple_of(j * ATT_TILE, ATT_TILE), ATT_TILE), :]
        z = lax.dot_general(k, q, _CONTRACT_LANES, preferred_element_type=F32)
        sp = jnp.maximum(z, 0.0) + jnp.log(1.0 + jnp.exp(-jnp.abs(z)))
        if diagonal:
            sp = jnp.where(valid, sp, 0.0)
        sp_hi = sp.astype(BF16)
        sp_lo = (sp - sp_hi.astype(F32)).astype(BF16)
        r_in = (jnp.dot(suffix, sp_hi, preferred_element_type=F32)
                + jnp.dot(suffix, sp_lo, preferred_element_type=F32))
        return z, r_in

    def back(j, z, r_in, off, keep):
        a = jnp.where(keep, jnp.exp(z - (r_in + off)), 0.0)
        acc_ref[...] += jnp.dot(vt_ref[j], a.astype(BF16), preferred_element_type=F32)

    has_prev = i > 0
    j_prev = jnp.maximum(i - 1, 0)
    z0, r0 = front(i, True)
    z1, r1 = front(j_prev, False)
    back(i, z0, r0, jnp.zeros((1, ATT_TILE), F32), valid)
    off = r0[0:1, :]
    back(j_prev, z1, r1, off, has_prev)
    off = off + jnp.where(has_prev, r1[0:1, :], 0.0)

    def cond(carry):
        n, _, min_off = carry
        return jnp.logical_and(n < i, min_off <= SB_UNDERFLOW)

    def body(carry):
        n, off, _ = carry
        j = i - 1 - n
        z, r_in = front(j, False)
        back(j, z, r_in, off, True)
        off = off + r_in[0:1, :]
        return n + 1, off, jnp.min(off)

    lax.while_loop(cond, body, (jnp.int32(1), off, jnp.min(off)))
    o_ref[...] = acc_ref[...].T.astype(BF16)


def _stick_breaking_attention(qk, vt):
    S = qk.shape[0]
    k_col0 = D_MODEL // LANES
    return pl.pallas_call(
        _sb_kernel,
        grid=(SB_HEADS, S // ATT_TILE),
        in_specs=[pl.BlockSpec((ATT_TILE, LANES), lambda h, i: (i, h)),
                  pl.BlockSpec((S, LANES), lambda h, i: (0, k_col0 + h)),
                  pl.BlockSpec((S // ATT_TILE, LANES, ATT_TILE), lambda h, i: (0, h, 0))],
        out_specs=pl.BlockSpec((ATT_TILE, LANES), lambda h, i: (i, h)),
        out_shape=jax.ShapeDtypeStruct((S, D_MODEL), BF16),
        scratch_shapes=[pltpu.VMEM((LANES, ATT_TILE), F32)],
        compiler_params=_params("parallel", "parallel"),
        name="stick_breaking",
    )(qk, qk, vt)


def _proj_ln_kernel(*refs, n_parts):
    o_refs = refs[:n_parts]
    w_refs = refs[n_parts:2 * n_parts]
    x_ref, g_ref, b_ref, out_ref = refs[2 * n_parts:]
    m = jnp.dot(o_refs[0][...], w_refs[0][...], preferred_element_type=F32)
    for o_ref, w_ref in zip(o_refs[1:], w_refs[1:]):
        m = m + jnp.dot(o_ref[...], w_ref[...], preferred_element_type=F32)
    out_ref[...] = _layer_norm(DEEPNORM_ALPHA * x_ref[...] + m, g_ref[...], b_ref[...])


def _proj_ln(o_parts, w_parts, x, g, b):
    S = x.shape[0]
    n = len(o_parts)
    in_specs = ([pl.BlockSpec((ROW_TILE, o.shape[1]), lambda i: (i, 0)) for o in o_parts]
                + [pl.BlockSpec(w.shape, lambda i: (0, 0)) for w in w_parts]
                + [pl.BlockSpec((ROW_TILE, D_MODEL), lambda i: (i, 0)),
                   pl.BlockSpec((1, D_MODEL), lambda i: (0, 0)),
                   pl.BlockSpec((1, D_MODEL), lambda i: (0, 0))])
    return pl.pallas_call(
        functools.partial(_proj_ln_kernel, n_parts=n),
        grid=(S // ROW_TILE,),
        in_specs=in_specs,
        out_specs=pl.BlockSpec((ROW_TILE, D_MODEL), lambda i: (i, 0)),
        out_shape=jax.ShapeDtypeStruct((S, D_MODEL), F32),
        compiler_params=_params("parallel"),
        name="out_proj_ln",
    )(*o_parts, *w_parts, x, g.reshape(1, D_MODEL), b.reshape(1, D_MODEL))


def _ffn_kernel(x_ref, wup_ref, cw_ref, cb_ref, wd_ref, g_ref, b_ref, out_ref,
                xb_ref, acc_ref, u_ref, h_ref, carry_ref):
    i = pl.program_id(0)
    rows = x_ref.shape[0]
    n_f = wd_ref.shape[0]

    @pl.when(i == 0)
    def _():
        carry_ref[...] = jnp.zeros_like(carry_ref)

    xb_ref[...] = x_ref[...].astype(BF16)
    acc_ref[...] = jnp.zeros_like(acc_ref)
    h_ref[...] = jnp.zeros_like(h_ref)

    def up(f):
        xb = xb_ref[...]
        return [jnp.dot(xb, wup_ref[part * n_f + f], preferred_element_type=F32) for part in range(2)]

    def load_u(f, slot, u):
        for part in range(2):
            u_ref[slot, part, 0:SUBLANES, :] = carry_ref[f, part]
            u_ref[slot, part, SUBLANES:SUBLANES + rows, :] = u[part]

    def down(f, slot):
        acc_ref[...] += jnp.dot(h_ref[slot], wd_ref[f], preferred_element_type=F32)

    def gate(f, slot):
        c = []
        for part in range(2):
            cw = cw_ref[part * n_f + f]
            acc = cb_ref[part * n_f + f]
            for tap in range(CONV_WIDTH):
                start = SUBLANES - (CONV_WIDTH - 1 - tap)
                acc = acc + cw[tap:tap + 1, :] * u_ref[slot, part, start:start + rows, :]
            c.append(acc)
            carry_ref[f, part] = u_ref[slot, part, rows:rows + SUBLANES, :]
        h_ref[slot] = (c[0] * (1.0 / (1.0 + jnp.exp(-c[0]))) * c[1]).astype(BF16)

    def trip(f, slot):
        u_next = up(f + 1)
        down(jnp.maximum(f - 1, 0), 1 - slot)
        gate(f, slot)
        load_u(f + 1, 1 - slot, u_next)

    load_u(0, 0, up(0))

    def body(k, carry):
        trip(2 * k, 0)
        trip(2 * k + 1, 1)
        return carry

    assert n_f % 2 == 1
    lax.fori_loop(0, (n_f - 1) // 2, body, 0)
    down(n_f - 2, 1)
    gate(n_f - 1, 0)
    down(n_f - 1, 0)
    out_ref[...] = _layer_norm(DEEPNORM_ALPHA * x_ref[...] + acc_ref[...], g_ref[...], b_ref[...])


def _conv_ffn_ln(x, w_up, conv_w, conv_b, w_down, g, b):
    S = x.shape[0]
    tm, tf = FFN_ROW_TILE, FFN_COL_TILE
    n_f = FFN_DIM // tf
    w_up_t = w_up.astype(BF16).reshape(D_MODEL, 2 * n_f, tf).transpose(1, 0, 2)
    conv_w_t = conv_w.reshape(CONV_WIDTH, 2 * n_f, tf).transpose(1, 0, 2)
    conv_b_t = conv_b.reshape(2 * n_f, 1, tf)
    w_down_t = w_down.astype(BF16).reshape(n_f, tf, D_MODEL)
    whole = lambda a: pl.BlockSpec(a.shape, lambda i: (0,) * a.ndim)
    g2, b2 = g.reshape(1, D_MODEL), b.reshape(1, D_MODEL)
    return pl.pallas_call(
        _ffn_kernel,
        grid=(S // tm,),
        in_specs=[pl.BlockSpec((tm, D_MODEL), lambda i: (i, 0)),
                  whole(w_up_t), whole(conv_w_t), whole(conv_b_t), whole(w_down_t), whole(g2), whole(b2)],
        out_specs=pl.BlockSpec((tm, D_MODEL), lambda i: (i, 0)),
        out_shape=jax.ShapeDtypeStruct((S, D_MODEL), F32),
        scratch_shapes=[pltpu.VMEM((tm, D_MODEL), BF16),
                        pltpu.VMEM((tm, D_MODEL), F32),
                        pltpu.VMEM((2, 2, tm + 2 * SUBLANES, tf), F32),
                        pltpu.VMEM((2, tm, tf), BF16),
                        pltpu.VMEM((n_f, 2, SUBLANES, tf), F32)],
        compiler_params=_params("arbitrary"),
        name="conv_ffn_ln",
    )(x, w_up_t, conv_w_t, conv_b_t, w_down_t, g2, b2)


def kernel(x, w_qkv_ab, w_o_ab, w_qkv_c, w_o_c, ln_mix_g, ln_mix_b, w_up, conv_w, conv_b, w_down,
           ln_ffn_g, ln_ffn_b):
    batch, S, _ = x.shape
    assert S % ROW_TILE == 0 and S % FFN_ROW_TILE == 0 and S // MOBA_BLOCK <= HEAD_DIM
    rope_tables = _rope_tables(S)
    moba_width = MOBA_HEADS * HEAD_DIM
    outs = []
    for bi in range(batch):
        h = x[bi]
        for layer in range(DEPTH):
            if layer % 2 == 0:
                qk, vt = _qkv_proj(h, w_qkv_ab[layer // 2], HEAD_DIM ** -0.5, rope_tables)
                w_o = w_o_ab[layer // 2].astype(BF16)
                o_parts = [_moba_attention(qk, vt), _dilated_attention(qk, vt)]
                w_parts = [w_o[:moba_width], w_o[moba_width:]]
            else:
                qk, vt = _qkv_proj(h, w_qkv_c[layer // 2], SB_HEAD_DIM ** -0.5)
                o_parts = [_stick_breaking_attention(qk, vt)]
                w_parts = [w_o_c[layer // 2].astype(BF16)]
            h = _proj_ln(o_parts, w_parts, h, ln_mix_g[layer], ln_mix_b[layer])
            h = _conv_ffn_ln(h, w_up[layer], conv_w[layer], conv_b[layer], w_down[layer],
                             ln_ffn_g[layer], ln_ffn_b[layer])
        outs.append(h)
    return jnp.stack(outs)
```

```python
import functools

import numpy as np
import jax
import jax.numpy as jnp
from jax import lax
from jax.experimental import pallas as pl
from jax.experimental.pallas import tpu as pltpu

D_MODEL = 1024
DEPTH = 4
HEAD_DIM = 64
N_HEADS = D_MODEL // HEAD_DIM
MOBA_HEADS = N_HEADS // 4
SB_HEAD_DIM = 128
SB_HEADS = D_MODEL // SB_HEAD_DIM
MOBA_BLOCK = 256
MOBA_TOPK = 3
DILATED_PAIRS = ((128, 1), (512, 4), (2048, 16))
FFN_DIM = ((8 * D_MODEL // 3 + 127) // 128) * 128
CONV_WIDTH = 3
ROPE_THETA = 10000.0
LN_EPS = 1e-5
DEEPNORM_ALPHA = (2 * DEPTH) ** 0.25

LANES = 128
SUBLANES = 8
VMEM_LIMIT_BYTES = 56 * 1024 * 1024

ROW_TILE = 512
FFN_ROW_TILE = 512
FFN_COL_TILE = 256
ATT_TILE = 256
MAX_DILATED_WINDOW = max(w for w, _ in DILATED_PAIRS)
DILATED_KEY_TILES = MAX_DILATED_WINDOW // ATT_TILE + 1

NEG = -1e30
LOG2_E = float(np.log2(np.e))
SB_UNDERFLOW = 104.0
F32 = jnp.float32
BF16 = jnp.bfloat16

_CONTRACT_LANES = (((1,), (1,)), ((), ()))


def _params(*semantics, flags=None):
    return pltpu.CompilerParams(dimension_semantics=semantics, vmem_limit_bytes=VMEM_LIMIT_BYTES,
                                flags=flags)


def _layer_norm(y, g, b):
    mu = jnp.mean(y, axis=-1, keepdims=True)
    yc = y - mu
    var = jnp.mean(yc * yc, axis=-1, keepdims=True)
    return yc * lax.rsqrt(var + LN_EPS) * g + b


def _qkv_kernel(*refs, rope, q_scale):
    if rope:
        x_ref, w_ref, wvt_ref, cos_ref, sin_ref, qk_ref, vt_ref = refs
    else:
        x_ref, w_ref, wvt_ref, qk_ref, vt_ref = refs
    xb = x_ref[...].astype(BF16)
    for part in range(2):
        y = jnp.dot(xb, w_ref[:, part * D_MODEL:(part + 1) * D_MODEL], preferred_element_type=F32)
        if rope:
            c = cos_ref[...]
            s = sin_ref[...]
            lane = lax.broadcasted_iota(jnp.int32, c.shape, 1)
            first_half = (lane % HEAD_DIM) < (HEAD_DIM // 2)
            for cb in range(D_MODEL // LANES):
                yc = y[:, cb * LANES:(cb + 1) * LANES]
                partner = jnp.where(first_half,
                                    pltpu.roll(yc, LANES - HEAD_DIM // 2, 1),
                                    pltpu.roll(yc, HEAD_DIM // 2, 1))
                yc = yc * c + partner * s
                if part == 0:
                    yc = yc * q_scale
                qk_ref[:, part * D_MODEL + cb * LANES:part * D_MODEL + (cb + 1) * LANES] = yc.astype(BF16)
        else:
            if part == 0:
                y = y * q_scale
            qk_ref[:, part * D_MODEL:(part + 1) * D_MODEL] = y.astype(BF16)
    vt = lax.dot_general(wvt_ref[...], xb, _CONTRACT_LANES,
                         preferred_element_type=F32).astype(BF16)
    for t in range(ROW_TILE // ATT_TILE):
        vt_ref[t] = vt[:, t * ATT_TILE:(t + 1) * ATT_TILE]


def _qkv_proj(x, w_qkv, q_scale, rope_tables=None):
    w_qk = w_qkv[:, :2 * D_MODEL].astype(BF16)
    w_vt = w_qkv[:, 2 * D_MODEL:].T.astype(BF16)
    S = x.shape[0]
    rope = rope_tables is not None
    tiles = ROW_TILE // ATT_TILE
    in_specs = [pl.BlockSpec((ROW_TILE, D_MODEL), lambda i: (i, 0)),
                pl.BlockSpec((D_MODEL, 2 * D_MODEL), lambda i: (0, 0)),
                pl.BlockSpec((D_MODEL, D_MODEL), lambda i: (0, 0))]
    args = [x, w_qk, w_vt]
    if rope:
        in_specs += [pl.BlockSpec((ROW_TILE, LANES), lambda i: (i, 0))] * 2
        args += list(rope_tables)
    return pl.pallas_call(
        functools.partial(_qkv_kernel, rope=rope, q_scale=q_scale),
        grid=(S // ROW_TILE,),
        in_specs=in_specs,
        out_specs=[pl.BlockSpec((ROW_TILE, 2 * D_MODEL), lambda i: (i, 0)),
                   pl.BlockSpec((tiles, D_MODEL, ATT_TILE), lambda i: (i, 0, 0))],
        out_shape=[jax.ShapeDtypeStruct((S, 2 * D_MODEL), BF16),
                   jax.ShapeDtypeStruct((S // ATT_TILE, D_MODEL, ATT_TILE), BF16)],
        compiler_params=_params("parallel"),
        name="qkv_proj",
    )(*args)


def _rope_tables(S):
    inv = 1.0 / (ROPE_THETA ** (jnp.arange(0, HEAD_DIM, 2, dtype=F32) / HEAD_DIM))
    ang = jnp.arange(S, dtype=F32)[:, None] * inv[None, :]
    cos, sin = jnp.cos(ang), jnp.sin(ang)
    cos_t = jnp.tile(jnp.concatenate([cos, cos], axis=-1), (1, LANES // HEAD_DIM))
    sin_t = jnp.tile(jnp.concatenate([-sin, sin], axis=-1), (1, LANES // HEAD_DIM))
    return cos_t, sin_t


def _softmax_step(s_t, m_old, l_old):
    m_new = jnp.maximum(m_old, jnp.max(s_t, axis=0, keepdims=True))
    p_t = jnp.exp2(s_t - m_new)
    alpha = jnp.exp2(m_old - m_new)
    l_new = alpha * l_old + jnp.sum(p_t, axis=0, keepdims=True)
    return p_t.astype(BF16), alpha, m_new, l_new


def _attention_sweep(first, last, scores, logits, values, acc_ref, s_ref, p_ref):
    acc_ref[...] = jnp.zeros_like(acc_ref)
    p_ref[...] = jnp.zeros_like(p_ref)
    stat0 = (jnp.full((1, ATT_TILE), -jnp.inf, F32), jnp.zeros((1, ATT_TILE), F32))
    one = jnp.ones((1, ATT_TILE), F32)
    unrolled = isinstance(first, int) and isinstance(last, int)

    def accumulate(t, slot, h, alpha):
        rows = slice(h * HEAD_DIM, (h + 1) * HEAD_DIM)
        acc_ref[rows, :] = alpha * acc_ref[rows, :] + jnp.dot(values(t, h), p_ref[slot, h],
                                                              preferred_element_type=F32)

    def softmax(t, slot, h, stat, is_last):
        p_t, alpha, m_new, l_new = _softmax_step(logits(t, h, s_ref[slot, h], is_last), *stat)
        p_ref[slot, h] = p_t
        return alpha, (m_new, l_new)

    def trip(t, slot, carry):
        stats, alphas = carry
        prev = max(t - 1, first) if unrolled else jnp.maximum(t - 1, first)
        s_next = [scores(t + 1, h) for h in range(2)]
        new_stats, new_alphas = [], []
        for h in range(2):
            accumulate(prev, 1 - slot, h, alphas[h])
            alpha, stat = softmax(t, slot, h, stats[h], False)
            s_ref[1 - slot, h] = s_next[h]
            new_stats.append(stat)
            new_alphas.append(alpha)
        return tuple(new_stats), tuple(new_alphas)

    def finish(slot, carry):
        stats, alphas = carry
        prev = max(last - 1, first) if unrolled else jnp.maximum(last - 1, first)
        inv = []
        for h in range(2):
            accumulate(prev, 1 - slot, h, alphas[h])
            alpha, (_, l_new) = softmax(last, slot, h, stats[h], True)
            accumulate(last, slot, h, alpha)
            inv.append(1.0 / l_new)
        return tuple(inv)

    for h in range(2):
        s_ref[0, h] = scores(first, h)
    carry = ((stat0, stat0), (one, one))
    if unrolled:
        for t in range(first, last):
            carry = trip(t, (t - first) % 2, carry)
        inv = finish((last - first) % 2, carry)
    else:
        n = last - first

        def pair(k, carry):
            t = first + 2 * k
            return trip(t + 1, 1, trip(t, 0, carry))

        carry = lax.fori_loop(0, n // 2, pair, carry)
        inv = lax.cond(n % 2 == 1,
                       lambda c: finish(1, trip(last - 1, 0, c)),
                       lambda c: finish(0, c), carry)
    scale = jnp.concatenate([jnp.broadcast_to(v, (HEAD_DIM, ATT_TILE)) for v in inv], axis=0)
    return acc_ref[...] * scale


def _moba_select_kernel(q_ref, k_ref, qa_ref, ka_ref, km_ref):
    i = pl.program_id(1)
    n_blocks = k_ref.shape[0] // MOBA_BLOCK

    @pl.when(i == 0)
    def _():
        km_ref[...] = jnp.zeros_like(km_ref)

        def body(b, carry):
            kb = k_ref[pl.ds(pl.multiple_of(b * MOBA_BLOCK, MOBA_BLOCK), MOBA_BLOCK), :].astype(F32)
            km_ref[pl.ds(b, 1), :] = jnp.sum(kb, axis=0, keepdims=True) * (1.0 / MOBA_BLOCK)
            return carry

        lax.fori_loop(0, n_blocks, body, 0)

    q = q_ref[...].astype(F32)
    kt = k_ref[pl.ds(pl.multiple_of(i * ATT_TILE, ATT_TILE), ATT_TILE), :].astype(F32)
    lane_i = lax.broadcasted_iota(jnp.int32, q.shape, 1)
    lane = lane_i.astype(F32)
    low = lane_i < HEAD_DIM
    blk = i.astype(F32)
    km = km_ref[...]
    onehot = jnp.where(lane_i - HEAD_DIM == i, 1.0, 0.0)
    q_sw = pltpu.roll(q, HEAD_DIM, 1)
    k_sw = pltpu.roll(kt, HEAD_DIM, 1)
    for h in range(2):
        qh = jnp.where(low if h == 0 else jnp.logical_not(low), q, 0.0)
        gate = lax.dot_general(qh, km, _CONTRACT_LANES, precision=lax.Precision.HIGHEST,
                               preferred_element_type=F32)
        g = jnp.where(lane < blk, gate, -jnp.inf)
        sel = lane == blk
        for _ in range(MOBA_TOPK):
            m = jnp.max(g, axis=1, keepdims=True)
            first = jnp.min(jnp.where(g == m, lane, float(LANES)), axis=1, keepdims=True)
            pick = jnp.logical_and(lane == first, m > -jnp.inf)
            sel = jnp.logical_or(sel, pick)
            g = jnp.where(pick, -jnp.inf, g)
        bias = pltpu.roll(jnp.where(sel, 0.0, NEG), HEAD_DIM, 1)
        qa_ref[h] = jnp.where(low, q if h == 0 else q_sw, bias).astype(BF16)
        ka_ref[h] = jnp.where(low, kt if h == 0 else k_sw, onehot).astype(BF16)


def _moba_flash_kernel(qa_ref, ka_ref, vt_ref, o_ref, acc_ref, s_ref, p_ref):
    i = pl.program_id(1)
    key = lax.broadcasted_iota(jnp.int32, (ATT_TILE, ATT_TILE), 0)
    qry = lax.broadcasted_iota(jnp.int32, (ATT_TILE, ATT_TILE), 1)

    def scores(j, h):
        off = pl.multiple_of(j * ATT_TILE, ATT_TILE)
        return lax.dot_general(ka_ref[h, pl.ds(off, ATT_TILE), :], qa_ref[h], _CONTRACT_LANES,
                               preferred_element_type=F32)

    def logits(j, h, s_t, is_last):
        return jnp.where(key <= qry, s_t, -jnp.inf) if is_last else s_t

    def values(j, h):
        return vt_ref[j, h * HEAD_DIM:(h + 1) * HEAD_DIM, :]

    o_t = _attention_sweep(0, i, scores, logits, values, acc_ref, s_ref, p_ref)
    o_ref[...] = o_t.T.astype(BF16)


def _moba_attention(qk, vt):
    S = qk.shape[0]
    n_pairs = MOBA_HEADS // 2
    k_col0 = D_MODEL // LANES
    aug = jax.ShapeDtypeStruct((MOBA_HEADS, S, LANES), BF16)
    qa, ka = pl.pallas_call(
        _moba_select_kernel,
        grid=(n_pairs, S // ATT_TILE),
        in_specs=[pl.BlockSpec((ATT_TILE, LANES), lambda p, i: (i, p)),
                  pl.BlockSpec((S, LANES), lambda p, i: (0, k_col0 + p))],
        out_specs=[pl.BlockSpec((2, ATT_TILE, LANES), lambda p, i: (p, i, 0))] * 2,
        out_shape=[aug, aug],
        scratch_shapes=[pltpu.VMEM((LANES, LANES), F32)],
        compiler_params=_params("parallel", "arbitrary"),
        name="moba_select",
    )(qk, qk)
    return pl.pallas_call(
        _moba_flash_kernel,
        grid=(n_pairs, S // ATT_TILE),
        in_specs=[pl.BlockSpec((2, ATT_TILE, LANES), lambda p, i: (p, i, 0)),
                  pl.BlockSpec((2, S, LANES), lambda p, i: (p, 0, 0)),
                  pl.BlockSpec((S // ATT_TILE, LANES, ATT_TILE), lambda p, i: (0, p, 0))],
        out_specs=pl.BlockSpec((ATT_TILE, LANES), lambda p, i: (i, p)),
        out_shape=jax.ShapeDtypeStruct((S, n_pairs * LANES), BF16),
        scratch_shapes=[pltpu.VMEM((LANES, ATT_TILE), F32),
                        pltpu.VMEM((2, 2, ATT_TILE, ATT_TILE), F32),
                        pltpu.VMEM((2, 2, ATT_TILE, ATT_TILE), BF16)],
        compiler_params=_params("parallel", "parallel"),
        name="moba_flash",
    )(qa, ka, vt)


def _dilated_bias_table():
    key = np.arange(ATT_TILE)[:, None]
    qry = np.arange(ATT_TILE)[None, :]
    tiles = []
    for t in range(DILATED_KEY_TILES):
        d = qry - key + ATT_TILE * (DILATED_KEY_TILES - 1 - t)
        count = np.zeros_like(d)
        for window, dil in DILATED_PAIRS:
            count += ((d >= 0) & (d <= window) & (d % dil == 0)).astype(d.dtype)
        tiles.append(np.where(count > 0, np.log2(np.maximum(count, 1)), NEG))
    tiles.append(np.full_like(tiles[0], NEG))
    return jnp.asarray(np.stack(tiles), dtype=F32)


def _dilated_kernel(q_ref, k_ref, vt_ref, tab_ref, o_ref, acc_ref, s_ref, p_ref, qh_ref):
    i = pl.program_id(1)
    q = q_ref[...]
    low = lax.broadcasted_iota(jnp.int32, q.shape, 1) < HEAD_DIM
    zero = jnp.zeros_like(q)
    qh_ref[0] = jnp.where(low, q, zero)
    qh_ref[1] = jnp.where(low, zero, q)
    last = DILATED_KEY_TILES - 1

    def tile(t):
        return jnp.maximum(i - last + t, 0)

    def scores(t, h):
        off = pl.multiple_of(tile(t) * ATT_TILE, ATT_TILE)
        return lax.dot_general(k_ref[pl.ds(off, ATT_TILE), :], qh_ref[h], _CONTRACT_LANES,
                               preferred_element_type=F32)

    def logits(t, h, s_t, is_last):
        return s_t + tab_ref[jnp.where(i - last + t < 0, DILATED_KEY_TILES, t)]

    def values(t, h):
        return vt_ref[tile(t), h * HEAD_DIM:(h + 1) * HEAD_DIM, :]

    o_t = _attention_sweep(0, last, scores, logits, values, acc_ref, s_ref, p_ref)
    o_ref[...] = o_t.T.astype(BF16)


def _dilated_attention(qk, vt):
    S = qk.shape[0]
    n_pairs = (N_HEADS - MOBA_HEADS) // 2
    q_col0 = MOBA_HEADS // 2
    k_col0 = D_MODEL // LANES + q_col0
    return pl.pallas_call(
        _dilated_kernel,
        grid=(n_pairs, S // ATT_TILE),
        in_specs=[pl.BlockSpec((ATT_TILE, LANES), lambda p, i: (i, q_col0 + p)),
                  pl.BlockSpec((S, LANES), lambda p, i: (0, k_col0 + p)),
                  pl.BlockSpec((S // ATT_TILE, LANES, ATT_TILE), lambda p, i: (0, q_col0 + p, 0)),
                  pl.BlockSpec((DILATED_KEY_TILES + 1, ATT_TILE, ATT_TILE), lambda p, i: (0, 0, 0))],
        out_specs=pl.BlockSpec((ATT_TILE, LANES), lambda p, i: (i, p)),
        out_shape=jax.ShapeDtypeStruct((S, n_pairs * LANES), BF16),
        scratch_shapes=[pltpu.VMEM((LANES, ATT_TILE), F32),
                        pltpu.VMEM((2, 2, ATT_TILE, ATT_TILE), F32),
                        pltpu.VMEM((2, 2, ATT_TILE, ATT_TILE), BF16),
                        pltpu.VMEM((2, ATT_TILE, LANES), BF16)],
        compiler_params=_params("parallel", "parallel"),
        name="dilated_attention",
    )(qk, qk, vt, _dilated_bias_table())


def _sb_kernel(q_ref, k_ref, vt_ref, o_ref, acc_ref):
    i = pl.program_id(1)
    q = q_ref[...]
    key = lax.broadcasted_iota(jnp.int32, (ATT_TILE, ATT_TILE), 0)
    qry = lax.broadcasted_iota(jnp.int32, (ATT_TILE, ATT_TILE), 1)
    suffix = jnp.where(qry >= key, 1.0, 0.0).astype(BF16)
    acc_ref[...] = jnp.zeros_like(acc_ref)

    valid = key < qry

    def front(j, diagonal):
        k = k_ref[pl.ds(pl.multiple_of(j * ATT_TILE, ATT_TILE), ATT_TILE), :]
        z = lax.dot_general(k, q, _CONTRACT_LANES, preferred_element_type=F32)
        sp = jnp.maximum(z, 0.0) + jnp.log(1.0 + jnp.exp(-jnp.abs(z)))
        if diagonal:
            sp = jnp.where(valid, sp, 0.0)
        sp_hi = sp.astype(BF16)
        sp_lo = (sp - sp_hi.astype(F32)).astype(BF16)
        r_in = (jnp.dot(suffix, sp_hi, preferred_element_type=F32)
                + jnp.dot(suffix, sp_lo, preferred_element_type=F32))
        return z, r_in

    def back(j, z, r_in, off, keep):
        a = jnp.where(keep, jnp.exp(z - (r_in + off)), 0.0)
        acc_ref[...] += jnp.dot(vt_ref[j], a.astype(BF16), preferred_element_type=F32)

    has_prev = i > 0
    j_prev = jnp.maximum(i - 1, 0)
    z0, r0 = front(i, True)
    z1, r1 = front(j_prev, False)
    back(i, z0, r0, jnp.zeros((1, ATT_TILE), F32), valid)
    off = r0[0:1, :]
    back(j_prev, z1, r1, off, has_prev)
    off = off + jnp.where(has_prev, r1[0:1, :], 0.0)

    def cond(carry):
        n, _, min_off = carry
        return jnp.logical_and(n < i, min_off <= SB_UNDERFLOW)

    def body(carry):
        n, off, _ = carry
        j = i - 1 - n
        z, r_in = front(j, False)
        back(j, z, r_in, off, True)
        off = off + r_in[0:1, :]
        return n + 1, off, jnp.min(off)

    lax.while_loop(cond, body, (jnp.int32(1), off, jnp.min(off)))
    o_ref[...] = acc_ref[...].T.astype(BF16)


def _stick_breaking_attention(qk, vt):
    S = qk.shape[0]
    k_col0 = D_MODEL // LANES
    return pl.pallas_call(
        _sb_kernel,
        grid=(SB_HEADS, S // ATT_TILE),
        in_specs=[pl.BlockSpec((ATT_TILE, LANES), lambda h, i: (i, h)),
                  pl.BlockSpec((S, LANES), lambda h, i: (0, k_col0 + h)),
                  pl.BlockSpec((S // ATT_TILE, LANES, ATT_TILE), lambda h, i: (0, h, 0))],
        out_specs=pl.BlockSpec((ATT_TILE, LANES), lambda h, i: (i, h)),
        out_shape=jax.ShapeDtypeStruct((S, D_MODEL), BF16),
        scratch_shapes=[pltpu.VMEM((LANES, ATT_TILE), F32)],
        compiler_params=_params("parallel", "parallel"),
        name="stick_breaking",
    )(qk, qk, vt)


def _proj_ln_kernel(*refs, n_parts):
    o_refs = refs[:n_parts]
    w_refs = refs[n_parts:2 * n_parts]
    x_ref, g_ref, b_ref, out_ref = refs[2 * n_parts:]
    m = jnp.dot(o_refs[0][...], w_refs[0][...], preferred_element_type=F32)
    for o_ref, w_ref in zip(o_refs[1:], w_refs[1:]):
        m = m + jnp.dot(o_ref[...], w_ref[...], preferred_element_type=F32)
    out_ref[...] = _layer_norm(DEEPNORM_ALPHA * x_ref[...] + m, g_ref[...], b_ref[...])


def _proj_ln(o_parts, w_parts, x, g, b):
    S = x.shape[0]
    n = len(o_parts)
    in_specs = ([pl.BlockSpec((ROW_TILE, o.shape[1]), lambda i: (i, 0)) for o in o_parts]
                + [pl.BlockSpec(w.shape, lambda i: (0, 0)) for w in w_parts]
                + [pl.BlockSpec((ROW_TILE, D_MODEL), lambda i: (i, 0)),
                   pl.BlockSpec((1, D_MODEL), lambda i: (0, 0)),
                   pl.BlockSpec((1, D_MODEL), lambda i: (0, 0))])
    return pl.pallas_call(
        functools.partial(_proj_ln_kernel, n_parts=n),
        grid=(S // ROW_TILE,),
        in_specs=in_specs,
        out_specs=pl.BlockSpec((ROW_TILE, D_MODEL), lambda i: (i, 0)),
        out_shape=jax.ShapeDtypeStruct((S, D_MODEL), F32),
        compiler_params=_params("parallel"),
        name="out_proj_ln",
    )(*o_parts, *w_parts, x, g.reshape(1, D_MODEL), b.reshape(1, D_MODEL))


def _ffn_kernel(x_ref, wup_ref, cw_ref, cb_ref, wd_ref, g_ref, b_ref, out_ref,
                xb_ref, acc_ref, u_ref, h_ref, carry_ref):
    i = pl.program_id(0)
    rows = x_ref.shape[0]
    n_f = wd_ref.shape[0]

    @pl.when(i == 0)
    def _():
        carry_ref[...] = jnp.zeros_like(carry_ref)

    xb_ref[...] = x_ref[...].astype(BF16)
    acc_ref[...] = jnp.zeros_like(acc_ref)
    h_ref[...] = jnp.zeros_like(h_ref)

    def up(f):
        xb = xb_ref[...]
        return [jnp.dot(xb, wup_ref[part * n_f + f], preferred_element_type=F32) for part in range(2)]

    def load_u(f, slot, u):
        for part in range(2):
            u_ref[slot, part, 0:SUBLANES, :] = carry_ref[f, part]
            u_ref[slot, part, SUBLANES:SUBLANES + rows, :] = u[part]

    def down(f, slot):
        acc_ref[...] += jnp.dot(h_ref[slot], wd_ref[f], preferred_element_type=F32)

    def gate(f, slot):
        c = []
        for part in range(2):
            cw = cw_ref[part * n_f + f]
            acc = cb_ref[part * n_f + f]
            for tap in range(CONV_WIDTH):
                start = SUBLANES - (CONV_WIDTH - 1 - tap)
                acc = acc + cw[tap:tap + 1, :] * u_ref[slot, part, start:start + rows, :]
            c.append(acc)
            carry_ref[f, part] = u_ref[slot, part, rows:rows + SUBLANES, :]
        h_ref[slot] = (c[0] * (1.0 / (1.0 + jnp.exp(-c[0]))) * c[1]).astype(BF16)

    def trip(f, slot):
        u_next = up(f + 1)
        down(jnp.maximum(f - 1, 0), 1 - slot)
        gate(f, slot)
        load_u(f + 1, 1 - slot, u_next)

    load_u(0, 0, up(0))

    def body(k, carry):
        trip(2 * k, 0)
        trip(2 * k + 1, 1)
        return carry

    assert n_f % 2 == 1
    lax.fori_loop(0, (n_f - 1) // 2, body, 0)
    down(n_f - 2, 1)
    gate(n_f - 1, 0)
    down(n_f - 1, 0)
    out_ref[...] = _layer_norm(DEEPNORM_ALPHA * x_ref[...] + acc_ref[...], g_ref[...], b_ref[...])


def _conv_ffn_ln(x, w_up, conv_w, conv_b, w_down, g, b):
    S = x.shape[0]
    tm, tf = FFN_ROW_TILE, FFN_COL_TILE
    n_f = FFN_DIM // tf
    w_up_t = w_up.astype(BF16).reshape(D_MODEL, 2 * n_f, tf).transpose(1, 0, 2)
    conv_w_t = conv_w.reshape(CONV_WIDTH, 2 * n_f, tf).transpose(1, 0, 2)
    conv_b_t = conv_b.reshape(2 * n_f, 1, tf)
    w_down_t = w_down.astype(BF16).reshape(n_f, tf, D_MODEL)
    whole = lambda a: pl.BlockSpec(a.shape, lambda i: (0,) * a.ndim)
    g2, b2 = g.reshape(1, D_MODEL), b.reshape(1, D_MODEL)
    return pl.pallas_call(
        _ffn_kernel,
        grid=(S // tm,),
        in_specs=[pl.BlockSpec((tm, D_MODEL), lambda i: (i, 0)),
                  whole(w_up_t), whole(conv_w_t), whole(conv_b_t), whole(w_down_t), whole(g2), whole(b2)],
        out_specs=pl.BlockSpec((tm, D_MODEL), lambda i: (i, 0)),
        out_shape=jax.ShapeDtypeStruct((S, D_MODEL), F32),
        scratch_shapes=[pltpu.VMEM((tm, D_MODEL), BF16),
                        pltpu.VMEM((tm, D_MODEL), F32),
                        pltpu.VMEM((2, 2, tm + 2 * SUBLANES, tf), F32),
                        pltpu.VMEM((2, tm, tf), BF16),
                        pltpu.VMEM((n_f, 2, SUBLANES, tf), F32)],
        compiler_params=_params("arbitrary"),
        name="conv_ffn_ln",
    )(x, w_up_t, conv_w_t, conv_b_t, w_down_t, g2, b2)


def kernel(x, w_qkv_ab, w_o_ab, w_qkv_c, w_o_c, ln_mix_g, ln_mix_b, w_up, conv_w, conv_b, w_down,
           ln_ffn_g, ln_ffn_b):
    batch, S, _ = x.shape
    assert S % ROW_TILE == 0 and S % FFN_ROW_TILE == 0 and S // MOBA_BLOCK <= HEAD_DIM
    rope_tables = _rope_tables(S)
    moba_width = MOBA_HEADS * HEAD_DIM
    outs = []
    for bi in range(batch):
        h = x[bi]
        for layer in range(DEPTH):
            if layer % 2 == 0:
                qk, vt = _qkv_proj(h, w_qkv_ab[layer // 2], HEAD_DIM ** -0.5 * LOG2_E, rope_tables)
                w_o = w_o_ab[layer // 2].astype(BF16)
                o_parts = [_moba_attention(qk, vt), _dilated_attention(qk, vt)]
                w_parts = [w_o[:moba_width], w_o[moba_width:]]
            else:
                qk, vt = _qkv_proj(h, w_qkv_c[layer // 2], SB_HEAD_DIM ** -0.5)
                o_parts = [_stick_breaking_attention(qk, vt)]
                w_parts = [w_o_c[layer // 2].astype(BF16)]
            h = _proj_ln(o_parts, w_parts, h, ln_mix_g[layer], ln_mix_b[layer])
            h = _conv_ffn_ln(h, w_up[layer], conv_w[layer], conv_b[layer], w_down[layer],
                             ln_ffn_g[layer], ln_ffn_b[layer])
        outs.append(h)
    return jnp.stack(outs)
```

```python
import functools

import numpy as np
import jax
import jax.numpy as jnp
from jax import lax
from jax.experimental import pallas as pl
from jax.experimental.pallas import tpu as pltpu

D_MODEL = 1024
DEPTH = 4
HEAD_DIM = 64
N_HEADS = D_MODEL // HEAD_DIM
MOBA_HEADS = N_HEADS // 4
SB_HEAD_DIM = 128
SB_HEADS = D_MODEL // SB_HEAD_DIM
MOBA_BLOCK = 256
MOBA_TOPK = 3
DILATED_PAIRS = ((128, 1), (512, 4), (2048, 16))
FFN_DIM = ((8 * D_MODEL // 3 + 127) // 128) * 128
CONV_WIDTH = 3
ROPE_THETA = 10000.0
LN_EPS = 1e-5
DEEPNORM_ALPHA = (2 * DEPTH) ** 0.25

LANES = 128
SUBLANES = 8
VMEM_LIMIT_BYTES = 56 * 1024 * 1024

ROW_TILE = 512
FFN_ROW_TILE = 512
FFN_COL_TILE = 256
ATT_TILE = 256
MAX_DILATED_WINDOW = max(w for w, _ in DILATED_PAIRS)
DILATED_KEY_TILES = MAX_DILATED_WINDOW // ATT_TILE + 1

NEG = -1e30
LOG2_E = float(np.log2(np.e))
SB_UNDERFLOW = 104.0
F32 = jnp.float32
BF16 = jnp.bfloat16

_CONTRACT_LANES = (((1,), (1,)), ((), ()))


def _params(*semantics, flags=None):
    return pltpu.CompilerParams(dimension_semantics=semantics, vmem_limit_bytes=VMEM_LIMIT_BYTES,
                                flags=flags)


def _layer_norm(y, g, b):
    mu = jnp.mean(y, axis=-1, keepdims=True)
    yc = y - mu
    var = jnp.mean(yc * yc, axis=-1, keepdims=True)
    return yc * lax.rsqrt(var + LN_EPS) * g + b


def _qkv_kernel(*refs, rope, q_scale):
    if rope:
        x_ref, w_ref, wvt_ref, cos_ref, sin_ref, qk_ref, vt_ref = refs
    else:
        x_ref, w_ref, wvt_ref, qk_ref, vt_ref = refs
    xb = x_ref[...].astype(BF16)
    for part in range(2):
        y = jnp.dot(xb, w_ref[:, part * D_MODEL:(part + 1) * D_MODEL], preferred_element_type=F32)
        if rope:
            c = cos_ref[...]
            s = sin_ref[...]
            lane = lax.broadcasted_iota(jnp.int32, c.shape, 1)
            first_half = (lane % HEAD_DIM) < (HEAD_DIM // 2)
            for cb in range(D_MODEL // LANES):
                yc = y[:, cb * LANES:(cb + 1) * LANES]
                partner = jnp.where(first_half,
                                    pltpu.roll(yc, LANES - HEAD_DIM // 2, 1),
                                    pltpu.roll(yc, HEAD_DIM // 2, 1))
                yc = yc * c + partner * s
                if part == 0:
                    yc = yc * q_scale
                qk_ref[:, part * D_MODEL + cb * LANES:part * D_MODEL + (cb + 1) * LANES] = yc.astype(BF16)
        else:
            if part == 0:
                y = y * q_scale
            qk_ref[:, part * D_MODEL:(part + 1) * D_MODEL] = y.astype(BF16)
    vt = lax.dot_general(wvt_ref[...], xb, _CONTRACT_LANES,
                         preferred_element_type=F32).astype(BF16)
    for t in range(ROW_TILE // ATT_TILE):
        vt_ref[t] = vt[:, t * ATT_TILE:(t + 1) * ATT_TILE]


def _qkv_proj(x, w_qkv, q_scale, rope_tables=None):
    w_qk = w_qkv[:, :2 * D_MODEL].astype(BF16)
    w_vt = w_qkv[:, 2 * D_MODEL:].T.astype(BF16)
    S = x.shape[0]
    rope = rope_tables is not None
    tiles = ROW_TILE // ATT_TILE
    in_specs = [pl.BlockSpec((ROW_TILE, D_MODEL), lambda i: (i, 0)),
                pl.BlockSpec((D_MODEL, 2 * D_MODEL), lambda i: (0, 0)),
                pl.BlockSpec((D_MODEL, D_MODEL), lambda i: (0, 0))]
    args = [x, w_qk, w_vt]
    if rope:
        in_specs += [pl.BlockSpec((ROW_TILE, LANES), lambda i: (i, 0))] * 2
        args += list(rope_tables)
    return pl.pallas_call(
        functools.partial(_qkv_kernel, rope=rope, q_scale=q_scale),
        grid=(S // ROW_TILE,),
        in_specs=in_specs,
        out_specs=[pl.BlockSpec((ROW_TILE, 2 * D_MODEL), lambda i: (i, 0)),
                   pl.BlockSpec((tiles, D_MODEL, ATT_TILE), lambda i: (i, 0, 0))],
        out_shape=[jax.ShapeDtypeStruct((S, 2 * D_MODEL), BF16),
                   jax.ShapeDtypeStruct((S // ATT_TILE, D_MODEL, ATT_TILE), BF16)],
        compiler_params=_params("parallel"),
        name="qkv_proj",
    )(*args)


def _rope_tables(S):
    inv = 1.0 / (ROPE_THETA ** (jnp.arange(0, HEAD_DIM, 2, dtype=F32) / HEAD_DIM))
    ang = jnp.arange(S, dtype=F32)[:, None] * inv[None, :]
    cos, sin = jnp.cos(ang), jnp.sin(ang)
    cos_t = jnp.tile(jnp.concatenate([cos, cos], axis=-1), (1, LANES // HEAD_DIM))
    sin_t = jnp.tile(jnp.concatenate([-sin, sin], axis=-1), (1, LANES // HEAD_DIM))
    return cos_t, sin_t


def _softmax_step(s_t, m_old, l_old):
    m_new = jnp.maximum(m_old, jnp.max(s_t, axis=0, keepdims=True))
    p_t = jnp.exp2(s_t - m_new)
    alpha = jnp.exp2(m_old - m_new)
    l_new = alpha * l_old + jnp.sum(p_t, axis=0, keepdims=True)
    return p_t.astype(BF16), alpha, m_new, l_new


def _attention_sweep(first, last, scores, logits, values, acc_ref, s_ref, p_ref):
    acc_ref[...] = jnp.zeros_like(acc_ref)
    p_ref[...] = jnp.zeros_like(p_ref)
    stat0 = (jnp.full((1, ATT_TILE), -jnp.inf, F32), jnp.zeros((1, ATT_TILE), F32))
    one = jnp.ones((1, ATT_TILE), F32)
    unrolled = isinstance(first, int) and isinstance(last, int)

    def accumulate(t, slot, h, alpha):
        rows = slice(h * HEAD_DIM, (h + 1) * HEAD_DIM)
        acc_ref[rows, :] = alpha * acc_ref[rows, :] + jnp.dot(values(t, h), p_ref[slot, h],
                                                              preferred_element_type=F32)

    def softmax(t, slot, h, stat, is_last):
        p_t, alpha, m_new, l_new = _softmax_step(logits(t, h, s_ref[slot, h], is_last), *stat)
        p_ref[slot, h] = p_t
        return alpha, (m_new, l_new)

    def trip(t, slot, carry):
        stats, alphas = carry
        prev = max(t - 1, first) if unrolled else jnp.maximum(t - 1, first)
        s_next = [scores(t + 1, h) for h in range(2)]
        new_stats, new_alphas = [], []
        for h in range(2):
            accumulate(prev, 1 - slot, h, alphas[h])
            alpha, stat = softmax(t, slot, h, stats[h], False)
            s_ref[1 - slot, h] = s_next[h]
            new_stats.append(stat)
            new_alphas.append(alpha)
        return tuple(new_stats), tuple(new_alphas)

    def finish(slot, carry):
        stats, alphas = carry
        prev = max(last - 1, first) if unrolled else jnp.maximum(last - 1, first)
        inv = []
        for h in range(2):
            accumulate(prev, 1 - slot, h, alphas[h])
            alpha, (_, l_new) = softmax(last, slot, h, stats[h], True)
            accumulate(last, slot, h, alpha)
            inv.append(1.0 / l_new)
        return tuple(inv)

    for h in range(2):
        s_ref[0, h] = scores(first, h)
    carry = ((stat0, stat0), (one, one))
    if unrolled:
        for t in range(first, last):
            carry = trip(t, (t - first) % 2, carry)
        inv = finish((last - first) % 2, carry)
    else:
        n = last - first

        def pair(k, carry):
            t = first + 2 * k
            return trip(t + 1, 1, trip(t, 0, carry))

        carry = lax.fori_loop(0, n // 2, pair, carry)
        inv = lax.cond(n % 2 == 1,
                       lambda c: finish(1, trip(last - 1, 0, c)),
                       lambda c: finish(0, c), carry)
    scale = jnp.concatenate([jnp.broadcast_to(v, (HEAD_DIM, ATT_TILE)) for v in inv], axis=0)
    return acc_ref[...] * scale


def _moba_select_kernel(q_ref, k_ref, qa_ref, ka_ref, km_ref):
    i = pl.program_id(1)
    n_blocks = k_ref.shape[0] // MOBA_BLOCK

    @pl.when(i == 0)
    def _():
        km_ref[...] = jnp.zeros_like(km_ref)

        def body(b, carry):
            kb = k_ref[pl.ds(pl.multiple_of(b * MOBA_BLOCK, MOBA_BLOCK), MOBA_BLOCK), :].astype(F32)
            km_ref[pl.ds(b, 1), :] = jnp.sum(kb, axis=0, keepdims=True) * (1.0 / MOBA_BLOCK)
            return carry

        lax.fori_loop(0, n_blocks, body, 0)

    q = q_ref[...].astype(F32)
    kt = k_ref[pl.ds(pl.multiple_of(i * ATT_TILE, ATT_TILE), ATT_TILE), :].astype(F32)
    lane_i = lax.broadcasted_iota(jnp.int32, q.shape, 1)
    lane = lane_i.astype(F32)
    low = lane_i < HEAD_DIM
    blk = i.astype(F32)
    km = km_ref[...]
    onehot = jnp.where(lane_i - HEAD_DIM == i, 1.0, 0.0)
    q_sw = pltpu.roll(q, HEAD_DIM, 1)
    k_sw = pltpu.roll(kt, HEAD_DIM, 1)
    for h in range(2):
        qh = jnp.where(low if h == 0 else jnp.logical_not(low), q, 0.0)
        gate = lax.dot_general(qh, km, _CONTRACT_LANES, precision=lax.Precision.HIGHEST,
                               preferred_element_type=F32)
        g = jnp.where(lane < blk, gate, -jnp.inf)
        sel = lane == blk
        for _ in range(MOBA_TOPK):
            m = jnp.max(g, axis=1, keepdims=True)
            first = jnp.min(jnp.where(g == m, lane, float(LANES)), axis=1, keepdims=True)
            pick = jnp.logical_and(lane == first, m > -jnp.inf)
            sel = jnp.logical_or(sel, pick)
            g = jnp.where(pick, -jnp.inf, g)
        bias = pltpu.roll(jnp.where(sel, 0.0, NEG), HEAD_DIM, 1)
        qa_ref[h] = jnp.where(low, q if h == 0 else q_sw, bias).astype(BF16)
        ka_ref[h] = jnp.where(low, kt if h == 0 else k_sw, onehot).astype(BF16)


def _moba_flash_kernel(qa_ref, ka_ref, vt_ref, o_ref, acc_ref, s_ref, p_ref):
    i = pl.program_id(1)
    key = lax.broadcasted_iota(jnp.int32, (ATT_TILE, ATT_TILE), 0)
    qry = lax.broadcasted_iota(jnp.int32, (ATT_TILE, ATT_TILE), 1)

    def scores(j, h):
        off = pl.multiple_of(j * ATT_TILE, ATT_TILE)
        return lax.dot_general(ka_ref[h, pl.ds(off, ATT_TILE), :], qa_ref[h], _CONTRACT_LANES,
                               preferred_element_type=F32)

    def logits(j, h, s_t, is_last):
        return jnp.where(key <= qry, s_t, -jnp.inf) if is_last else s_t

    def values(j, h):
        return vt_ref[j, h * HEAD_DIM:(h + 1) * HEAD_DIM, :]

    o_t = _attention_sweep(0, i, scores, logits, values, acc_ref, s_ref, p_ref)
    o_ref[...] = o_t.T.astype(BF16)


def _moba_attention(qk, vt):
    S = qk.shape[0]
    n_pairs = MOBA_HEADS // 2
    k_col0 = D_MODEL // LANES
    aug = jax.ShapeDtypeStruct((MOBA_HEADS, S, LANES), BF16)
    qa, ka = pl.pallas_call(
        _moba_select_kernel,
        grid=(n_pairs, S // ATT_TILE),
        in_specs=[pl.BlockSpec((ATT_TILE, LANES), lambda p, i: (i, p)),
                  pl.BlockSpec((S, LANES), lambda p, i: (0, k_col0 + p))],
        out_specs=[pl.BlockSpec((2, ATT_TILE, LANES), lambda p, i: (p, i, 0))] * 2,
        out_shape=[aug, aug],
        scratch_shapes=[pltpu.VMEM((LANES, LANES), F32)],
        compiler_params=_params("parallel", "arbitrary"),
        name="moba_select",
    )(qk, qk)
    return pl.pallas_call(
        _moba_flash_kernel,
        grid=(n_pairs, S // ATT_TILE),
        in_specs=[pl.BlockSpec((2, ATT_TILE, LANES), lambda p, i: (p, i, 0)),
                  pl.BlockSpec((2, S, LANES), lambda p, i: (p, 0, 0)),
                  pl.BlockSpec((S // ATT_TILE, LANES, ATT_TILE), lambda p, i: (0, p, 0))],
        out_specs=pl.BlockSpec((ATT_TILE, LANES), lambda p, i: (i, p)),
        out_shape=jax.ShapeDtypeStruct((S, n_pairs * LANES), BF16),
        scratch_shapes=[pltpu.VMEM((LANES, ATT_TILE), F32),
                        pltpu.VMEM((2, 2, ATT_TILE, ATT_TILE), F32),
                        pltpu.VMEM((2, 2, ATT_TILE, ATT_TILE), BF16)],
        compiler_params=_params("parallel", "parallel"),
        name="moba_flash",
    )(qa, ka, vt)


def _dilated_bias_table():
    key = np.arange(ATT_TILE)[:, None]
    qry = np.arange(ATT_TILE)[None, :]
    tiles = []
    for t in range(DILATED_KEY_TILES):
        d = qry - key + ATT_TILE * (DILATED_KEY_TILES - 1 - t)
        count = np.zeros_like(d)
        for window, dil in DILATED_PAIRS:
            count += ((d >= 0) & (d <= window) & (d % dil == 0)).astype(d.dtype)
        tiles.append(np.where(count > 0, np.log2(np.maximum(count, 1)), NEG))
    tiles.append(np.full_like(tiles[0], NEG))
    return jnp.asarray(np.stack(tiles), dtype=F32)


def _dilated_kernel(q_ref, k_ref, vt_ref, tab_ref, o_ref, acc_ref, s_ref, p_ref, qh_ref):
    i = pl.program_id(1)
    q = q_ref[...]
    low = lax.broadcasted_iota(jnp.int32, q.shape, 1) < HEAD_DIM
    zero = jnp.zeros_like(q)
    qh_ref[0] = jnp.where(low, q, zero)
    qh_ref[1] = jnp.where(low, zero, q)
    last = DILATED_KEY_TILES - 1

    def tile(t):
        return jnp.maximum(i - last + t, 0)

    def scores(t, h):
        off = pl.multiple_of(tile(t) * ATT_TILE, ATT_TILE)
        return lax.dot_general(k_ref[pl.ds(off, ATT_TILE), :], qh_ref[h], _CONTRACT_LANES,
                               preferred_element_type=F32)

    def logits(t, h, s_t, is_last):
        return s_t + tab_ref[jnp.where(i - last + t < 0, DILATED_KEY_TILES, t)]

    def values(t, h):
        return vt_ref[tile(t), h * HEAD_DIM:(h + 1) * HEAD_DIM, :]

    o_t = _attention_sweep(0, last, scores, logits, values, acc_ref, s_ref, p_ref)
    o_ref[...] = o_t.T.astype(BF16)


def _dilated_attention(qk, vt):
    S = qk.shape[0]
    n_pairs = (N_HEADS - MOBA_HEADS) // 2
    q_col0 = MOBA_HEADS // 2
    k_col0 = D_MODEL // LANES + q_col0
    return pl.pallas_call(
        _dilated_kernel,
        grid=(n_pairs, S // ATT_TILE),
        in_specs=[pl.BlockSpec((ATT_TILE, LANES), lambda p, i: (i, q_col0 + p)),
                  pl.BlockSpec((S, LANES), lambda p, i: (0, k_col0 + p)),
                  pl.BlockSpec((S // ATT_TILE, LANES, ATT_TILE), lambda p, i: (0, q_col0 + p, 0)),
                  pl.BlockSpec((DILATED_KEY_TILES + 1, ATT_TILE, ATT_TILE), lambda p, i: (0, 0, 0))],
        out_specs=pl.BlockSpec((ATT_TILE, LANES), lambda p, i: (i, p)),
        out_shape=jax.ShapeDtypeStruct((S, n_pairs * LANES), BF16),
        scratch_shapes=[pltpu.VMEM((LANES, ATT_TILE), F32),
                        pltpu.VMEM((2, 2, ATT_TILE, ATT_TILE), F32),
                        pltpu.VMEM((2, 2, ATT_TILE, ATT_TILE), BF16),
                        pltpu.VMEM((2, ATT_TILE, LANES), BF16)],
        compiler_params=_params("parallel", "parallel"),
        name="dilated_attention",
    )(qk, qk, vt, _dilated_bias_table())


SB_HEADS_PER_STEP = 2


def _sb_kernel(q_ref, k_ref, vt_ref, o_ref, acc_ref):
    i = pl.program_id(1)
    heads = range(SB_HEADS_PER_STEP)
    key = lax.broadcasted_iota(jnp.int32, (ATT_TILE, ATT_TILE), 0)
    qry = lax.broadcasted_iota(jnp.int32, (ATT_TILE, ATT_TILE), 1)
    suffix = jnp.where(qry >= key, 1.0, 0.0).astype(BF16)
    acc_ref[...] = jnp.zeros_like(acc_ref)
    valid = key < qry

    def lanes(h):
        return slice(h * LANES, (h + 1) * LANES)

    def front(j, h, diagonal):
        k = k_ref[pl.ds(pl.multiple_of(j * ATT_TILE, ATT_TILE), ATT_TILE), lanes(h)]
        z = lax.dot_general(k, q_ref[:, lanes(h)], _CONTRACT_LANES, preferred_element_type=F32)
        sp = jnp.maximum(z, 0.0) + jnp.log(1.0 + jnp.exp(-jnp.abs(z)))
        if diagonal:
            sp = jnp.where(valid, sp, 0.0)
        sp_hi = sp.astype(BF16)
        sp_lo = (sp - sp_hi.astype(F32)).astype(BF16)
        r_in = (jnp.dot(suffix, sp_hi, preferred_element_type=F32)
                + jnp.dot(suffix, sp_lo, preferred_element_type=F32))
        return z, r_in

    def back(j, h, z, r_in, off, keep):
        a = jnp.where(keep, jnp.exp(z - (r_in + off)), 0.0)
        acc_ref[lanes(h), :] += jnp.dot(vt_ref[j, lanes(h), :], a.astype(BF16), preferred_element_type=F32)

    has_prev = i > 0
    j_prev = jnp.maximum(i - 1, 0)
    own = [front(i, h, True) for h in heads]
    prev = [front(j_prev, h, False) for h in heads]
    offs = []
    for h in heads:
        (z0, r0), (z1, r1) = own[h], prev[h]
        back(i, h, z0, r0, jnp.zeros((1, ATT_TILE), F32), valid)
        off = r0[0:1, :]
        back(j_prev, h, z1, r1, off, has_prev)
        offs.append(off + jnp.where(has_prev, r1[0:1, :], 0.0))
    offs = tuple(offs)

    def least(offs):
        return functools.reduce(jnp.minimum, [jnp.min(off) for off in offs])

    def cond(carry):
        n, _, min_off = carry
        return jnp.logical_and(n < i, min_off <= SB_UNDERFLOW)

    def body(carry):
        n, offs, _ = carry
        j = i - 1 - n
        new = []
        for h in heads:
            z, r_in = front(j, h, False)
            back(j, h, z, r_in, offs[h], True)
            new.append(offs[h] + r_in[0:1, :])
        return n + 1, tuple(new), least(new)

    lax.while_loop(cond, body, (jnp.int32(1), offs, least(offs)))
    o_ref[...] = acc_ref[...].T.astype(BF16)


def _stick_breaking_attention(qk, vt):
    S = qk.shape[0]
    width = SB_HEADS_PER_STEP * LANES
    k_col0 = D_MODEL // width
    return pl.pallas_call(
        _sb_kernel,
        grid=(SB_HEADS // SB_HEADS_PER_STEP, S // ATT_TILE),
        in_specs=[pl.BlockSpec((ATT_TILE, width), lambda g, i: (i, g)),
                  pl.BlockSpec((S, width), lambda g, i: (0, k_col0 + g)),
                  pl.BlockSpec((S // ATT_TILE, width, ATT_TILE), lambda g, i: (0, g, 0))],
        out_specs=pl.BlockSpec((ATT_TILE, width), lambda g, i: (i, g)),
        out_shape=jax.ShapeDtypeStruct((S, D_MODEL), BF16),
        scratch_shapes=[pltpu.VMEM((width, ATT_TILE), F32)],
        compiler_params=_params("parallel", "parallel"),
        name="stick_breaking",
    )(qk, qk, vt)


def _proj_ln_kernel(*refs, n_parts):
    o_refs = refs[:n_parts]
    w_refs = refs[n_parts:2 * n_parts]
    x_ref, g_ref, b_ref, out_ref = refs[2 * n_parts:]
    m = jnp.dot(o_refs[0][...], w_refs[0][...], preferred_element_type=F32)
    for o_ref, w_ref in zip(o_refs[1:], w_refs[1:]):
        m = m + jnp.dot(o_ref[...], w_ref[...], preferred_element_type=F32)
    out_ref[...] = _layer_norm(DEEPNORM_ALPHA * x_ref[...] + m, g_ref[...], b_ref[...])


def _proj_ln(o_parts, w_parts, x, g, b):
    S = x.shape[0]
    n = len(o_parts)
    in_specs = ([pl.BlockSpec((ROW_TILE, o.shape[1]), lambda i: (i, 0)) for o in o_parts]
                + [pl.BlockSpec(w.shape, lambda i: (0, 0)) for w in w_parts]
                + [pl.BlockSpec((ROW_TILE, D_MODEL), lambda i: (i, 0)),
                   pl.BlockSpec((1, D_MODEL), lambda i: (0, 0)),
                   pl.BlockSpec((1, D_MODEL), lambda i: (0, 0))])
    return pl.pallas_call(
        functools.partial(_proj_ln_kernel, n_parts=n),
        grid=(S // ROW_TILE,),
        in_specs=in_specs,
        out_specs=pl.BlockSpec((ROW_TILE, D_MODEL), lambda i: (i, 0)),
        out_shape=jax.ShapeDtypeStruct((S, D_MODEL), F32),
        compiler_params=_params("parallel"),
        name="out_proj_ln",
    )(*o_parts, *w_parts, x, g.reshape(1, D_MODEL), b.reshape(1, D_MODEL))


def _ffn_kernel(x_ref, wup_ref, cw_ref, cb_ref, wd_ref, g_ref, b_ref, out_ref,
                xb_ref, acc_ref, u_ref, h_ref, carry_ref):
    i = pl.program_id(0)
    rows = x_ref.shape[0]
    n_f = wd_ref.shape[0]

    @pl.when(i == 0)
    def _():
        carry_ref[...] = jnp.zeros_like(carry_ref)

    xb_ref[...] = x_ref[...].astype(BF16)
    acc_ref[...] = jnp.zeros_like(acc_ref)
    h_ref[...] = jnp.zeros_like(h_ref)

    def up(f):
        xb = xb_ref[...]
        return [jnp.dot(xb, wup_ref[part * n_f + f], preferred_element_type=F32) for part in range(2)]

    def load_u(f, slot, u):
        for part in range(2):
            u_ref[slot, part, 0:SUBLANES, :] = carry_ref[f, part]
            u_ref[slot, part, SUBLANES:SUBLANES + rows, :] = u[part]

    def down(f, slot):
        acc_ref[...] += jnp.dot(h_ref[slot], wd_ref[f], preferred_element_type=F32)

    def gate(f, slot):
        c = []
        for part in range(2):
            cw = cw_ref[part * n_f + f]
            acc = cb_ref[part * n_f + f]
            for tap in range(CONV_WIDTH):
                start = SUBLANES - (CONV_WIDTH - 1 - tap)
                acc = acc + cw[tap:tap + 1, :] * u_ref[slot, part, start:start + rows, :]
            c.append(acc)
            carry_ref[f, part] = u_ref[slot, part, rows:rows + SUBLANES, :]
        h_ref[slot] = (c[0] * (1.0 / (1.0 + jnp.exp(-c[0]))) * c[1]).astype(BF16)

    def trip(f, slot):
        u_next = up(f + 1)
        down(jnp.maximum(f - 1, 0), 1 - slot)
        gate(f, slot)
        load_u(f + 1, 1 - slot, u_next)

    load_u(0, 0, up(0))

    def body(k, carry):
        trip(2 * k, 0)
        trip(2 * k + 1, 1)
        return carry

    assert n_f % 2 == 1
    lax.fori_loop(0, (n_f - 1) // 2, body, 0, unroll=True)
    down(n_f - 2, 1)
    gate(n_f - 1, 0)
    down(n_f - 1, 0)
    out_ref[...] = _layer_norm(DEEPNORM_ALPHA * x_ref[...] + acc_ref[...], g_ref[...], b_ref[...])


def _conv_ffn_ln(x, w_up, conv_w, conv_b, w_down, g, b):
    S = x.shape[0]
    tm, tf = FFN_ROW_TILE, FFN_COL_TILE
    n_f = FFN_DIM // tf
    w_up_t = w_up.astype(BF16).reshape(D_MODEL, 2 * n_f, tf).transpose(1, 0, 2)
    conv_w_t = conv_w.reshape(CONV_WIDTH, 2 * n_f, tf).transpose(1, 0, 2)
    conv_b_t = conv_b.reshape(2 * n_f, 1, tf)
    w_down_t = w_down.astype(BF16).reshape(n_f, tf, D_MODEL)
    whole = lambda a: pl.BlockSpec(a.shape, lambda i: (0,) * a.ndim)
    g2, b2 = g.reshape(1, D_MODEL), b.reshape(1, D_MODEL)
    return pl.pallas_call(
        _ffn_kernel,
        grid=(S // tm,),
        in_specs=[pl.BlockSpec((tm, D_MODEL), lambda i: (i, 0)),
                  whole(w_up_t), whole(conv_w_t), whole(conv_b_t), whole(w_down_t), whole(g2), whole(b2)],
        out_specs=pl.BlockSpec((tm, D_MODEL), lambda i: (i, 0)),
        out_shape=jax.ShapeDtypeStruct((S, D_MODEL), F32),
        scratch_shapes=[pltpu.VMEM((tm, D_MODEL), BF16),
                        pltpu.VMEM((tm, D_MODEL), F32),
                        pltpu.VMEM((2, 2, tm + 2 * SUBLANES, tf), F32),
                        pltpu.VMEM((2, tm, tf), BF16),
                        pltpu.VMEM((n_f, 2, SUBLANES, tf), F32)],
        compiler_params=_params("arbitrary"),
        name="conv_ffn_ln",
    )(x, w_up_t, conv_w_t, conv_b_t, w_down_t, g2, b2)


def kernel(x, w_qkv_ab, w_o_ab, w_qkv_c, w_o_c, ln_mix_g, ln_mix_b, w_up, conv_w, conv_b, w_down,
           ln_ffn_g, ln_ffn_b):
    batch, S, _ = x.shape
    assert S % ROW_TILE == 0 and S % FFN_ROW_TILE == 0 and S // MOBA_BLOCK <= HEAD_DIM
    rope_tables = _rope_tables(S)
    moba_width = MOBA_HEADS * HEAD_DIM
    outs = []
    for bi in range(batch):
        h = x[bi]
        for layer in range(DEPTH):
            if layer % 2 == 0:
                qk, vt = _qkv_proj(h, w_qkv_ab[layer // 2], HEAD_DIM ** -0.5 * LOG2_E, rope_tables)
                w_o = w_o_ab[layer // 2].astype(BF16)
                o_parts = [_moba_attention(qk, vt), _dilated_attention(qk, vt)]
                w_parts = [w_o[:moba_width], w_o[moba_width:]]
            else:
                qk, vt = _qkv_proj(h, w_qkv_c[layer // 2], SB_HEAD_DIM ** -0.5)
                o_parts = [_stick_breaking_attention(qk, vt)]
                w_parts = [w_o_c[layer // 2].astype(BF16)]
            h = _proj_ln(o_parts, w_parts, h, ln_mix_g[layer], ln_mix_b[layer])
            h = _conv_ffn_ln(h, w_up[layer], conv_w[layer], conv_b[layer], w_down[layer],
                             ln_ffn_g[layer], ln_ffn_b[layer])
        outs.append(h)
    return jnp.stack(outs)
```

```python
import functools

import numpy as np
import jax
import jax.numpy as jnp
from jax import lax
from jax.experimental import pallas as pl
from jax.experimental.pallas import tpu as pltpu

D_MODEL = 1024
DEPTH = 4
HEAD_DIM = 64
N_HEADS = D_MODEL // HEAD_DIM
MOBA_HEADS = N_HEADS // 4
SB_HEAD_DIM = 128
SB_HEADS = D_MODEL // SB_HEAD_DIM
MOBA_BLOCK = 256
MOBA_TOPK = 3
DILATED_PAIRS = ((128, 1), (512, 4), (2048, 16))
FFN_DIM = ((8 * D_MODEL // 3 + 127) // 128) * 128
CONV_WIDTH = 3
ROPE_THETA = 10000.0
LN_EPS = 1e-5
DEEPNORM_ALPHA = (2 * DEPTH) ** 0.25

LANES = 128
SUBLANES = 8
VMEM_LIMIT_BYTES = 56 * 1024 * 1024

ROW_TILE = 512
FFN_ROW_TILE = 512
FFN_COL_TILE = 256
ATT_TILE = 256
MAX_DILATED_WINDOW = max(w for w, _ in DILATED_PAIRS)
DILATED_KEY_TILES = MAX_DILATED_WINDOW // ATT_TILE + 1

NEG = -1e30
LOG2_E = float(np.log2(np.e))
SB_UNDERFLOW = 104.0
F32 = jnp.float32
BF16 = jnp.bfloat16

_CONTRACT_LANES = (((1,), (1,)), ((), ()))


def _params(*semantics, flags=None):
    return pltpu.CompilerParams(dimension_semantics=semantics, vmem_limit_bytes=VMEM_LIMIT_BYTES,
                                flags=flags)


def _layer_norm(y, g, b):
    mu = jnp.mean(y, axis=-1, keepdims=True)
    yc = y - mu
    var = jnp.mean(yc * yc, axis=-1, keepdims=True)
    return yc * lax.rsqrt(var + LN_EPS) * g + b


def _qkv_kernel(*refs, rope, q_scale):
    if rope:
        x_ref, w_ref, wvt_ref, cos_ref, sin_ref, qk_ref, vt_ref = refs
    else:
        x_ref, w_ref, wvt_ref, qk_ref, vt_ref = refs
    xb = x_ref[...].astype(BF16)
    for part in range(2):
        y = jnp.dot(xb, w_ref[:, part * D_MODEL:(part + 1) * D_MODEL], preferred_element_type=F32)
        if rope:
            c = cos_ref[...]
            s = sin_ref[...]
            lane = lax.broadcasted_iota(jnp.int32, c.shape, 1)
            first_half = (lane % HEAD_DIM) < (HEAD_DIM // 2)
            for cb in range(D_MODEL // LANES):
                yc = y[:, cb * LANES:(cb + 1) * LANES]
                partner = jnp.where(first_half,
                                    pltpu.roll(yc, LANES - HEAD_DIM // 2, 1),
                                    pltpu.roll(yc, HEAD_DIM // 2, 1))
                yc = yc * c + partner * s
                if part == 0:
                    yc = yc * q_scale
                qk_ref[:, part * D_MODEL + cb * LANES:part * D_MODEL + (cb + 1) * LANES] = yc.astype(BF16)
        else:
            if part == 0:
                y = y * q_scale
            qk_ref[:, part * D_MODEL:(part + 1) * D_MODEL] = y.astype(BF16)
    vt = lax.dot_general(wvt_ref[...], xb, _CONTRACT_LANES,
                         preferred_element_type=F32).astype(BF16)
    for t in range(ROW_TILE // ATT_TILE):
        vt_ref[t] = vt[:, t * ATT_TILE:(t + 1) * ATT_TILE]


def _qkv_proj(x, w_qkv, q_scale, rope_tables=None):
    w_qk = w_qkv[:, :2 * D_MODEL].astype(BF16)
    w_vt = w_qkv[:, 2 * D_MODEL:].T.astype(BF16)
    S = x.shape[0]
    rope = rope_tables is not None
    tiles = ROW_TILE // ATT_TILE
    in_specs = [pl.BlockSpec((ROW_TILE, D_MODEL), lambda i: (i, 0)),
                pl.BlockSpec((D_MODEL, 2 * D_MODEL), lambda i: (0, 0)),
                pl.BlockSpec((D_MODEL, D_MODEL), lambda i: (0, 0))]
    args = [x, w_qk, w_vt]
    if rope:
        in_specs += [pl.BlockSpec((ROW_TILE, LANES), lambda i: (i, 0))] * 2
        args += list(rope_tables)
    return pl.pallas_call(
        functools.partial(_qkv_kernel, rope=rope, q_scale=q_scale),
        grid=(S // ROW_TILE,),
        in_specs=in_specs,
        out_specs=[pl.BlockSpec((ROW_TILE, 2 * D_MODEL), lambda i: (i, 0)),
                   pl.BlockSpec((tiles, D_MODEL, ATT_TILE), lambda i: (i, 0, 0))],
        out_shape=[jax.ShapeDtypeStruct((S, 2 * D_MODEL), BF16),
                   jax.ShapeDtypeStruct((S // ATT_TILE, D_MODEL, ATT_TILE), BF16)],
        compiler_params=_params("parallel"),
        name="qkv_proj",
    )(*args)


def _rope_tables(S):
    inv = 1.0 / (ROPE_THETA ** (jnp.arange(0, HEAD_DIM, 2, dtype=F32) / HEAD_DIM))
    ang = jnp.arange(S, dtype=F32)[:, None] * inv[None, :]
    cos, sin = jnp.cos(ang), jnp.sin(ang)
    cos_t = jnp.tile(jnp.concatenate([cos, cos], axis=-1), (1, LANES // HEAD_DIM))
    sin_t = jnp.tile(jnp.concatenate([-sin, sin], axis=-1), (1, LANES // HEAD_DIM))
    return cos_t, sin_t


def _softmax_step(s_t, m_old, l_old):
    m_new = jnp.maximum(m_old, jnp.max(s_t, axis=0, keepdims=True))
    p_t = jnp.exp2(s_t - m_new)
    alpha = jnp.exp2(m_old - m_new)
    l_new = alpha * l_old + jnp.sum(p_t, axis=0, keepdims=True)
    return p_t.astype(BF16), alpha, m_new, l_new


def _attention_sweep(first, last, scores, logits, values, acc_ref, s_ref, p_ref):
    acc_ref[...] = jnp.zeros_like(acc_ref)
    p_ref[...] = jnp.zeros_like(p_ref)
    stat0 = (jnp.full((1, ATT_TILE), -jnp.inf, F32), jnp.zeros((1, ATT_TILE), F32))
    one = jnp.ones((1, ATT_TILE), F32)
    unrolled = isinstance(first, int) and isinstance(last, int)

    def accumulate(t, slot, h, alpha):
        rows = slice(h * HEAD_DIM, (h + 1) * HEAD_DIM)
        acc_ref[rows, :] = alpha * acc_ref[rows, :] + jnp.dot(values(t, h), p_ref[slot, h],
                                                              preferred_element_type=F32)

    def softmax(t, slot, h, stat, is_last):
        p_t, alpha, m_new, l_new = _softmax_step(logits(t, h, s_ref[slot, h], is_last), *stat)
        p_ref[slot, h] = p_t
        return alpha, (m_new, l_new)

    def trip(t, slot, carry):
        stats, alphas = carry
        prev = max(t - 1, first) if unrolled else jnp.maximum(t - 1, first)
        s_next = [scores(t + 1, h) for h in range(2)]
        new_stats, new_alphas = [], []
        for h in range(2):
            accumulate(prev, 1 - slot, h, alphas[h])
            alpha, stat = softmax(t, slot, h, stats[h], False)
            s_ref[1 - slot, h] = s_next[h]
            new_stats.append(stat)
            new_alphas.append(alpha)
        return tuple(new_stats), tuple(new_alphas)

    def finish(slot, carry):
        stats, alphas = carry
        prev = max(last - 1, first) if unrolled else jnp.maximum(last - 1, first)
        inv = []
        for h in range(2):
            accumulate(prev, 1 - slot, h, alphas[h])
            alpha, (_, l_new) = softmax(last, slot, h, stats[h], True)
            accumulate(last, slot, h, alpha)
            inv.append(1.0 / l_new)
        return tuple(inv)

    for h in range(2):
        s_ref[0, h] = scores(first, h)
    carry = ((stat0, stat0), (one, one))
    if unrolled:
        for t in range(first, last):
            carry = trip(t, (t - first) % 2, carry)
        inv = finish((last - first) % 2, carry)
    else:
        n = last - first

        def quad(k, carry):
            t = first + 4 * k
            for d in range(4):
                carry = trip(t + d, d % 2, carry)
            return carry

        def pair(k, carry):
            t = first + (n // 4) * 4 + 2 * k
            return trip(t + 1, 1, trip(t, 0, carry))

        carry = lax.fori_loop(0, n // 4, quad, carry)
        carry = lax.fori_loop(0, (n % 4) // 2, pair, carry)
        inv = lax.cond(n % 2 == 1,
                       lambda c: finish(1, trip(last - 1, 0, c)),
                       lambda c: finish(0, c), carry)
    scale = jnp.concatenate([jnp.broadcast_to(v, (HEAD_DIM, ATT_TILE)) for v in inv], axis=0)
    return acc_ref[...] * scale


def _moba_select_kernel(q_ref, k_ref, qa_ref, ka_ref, km_ref):
    i = pl.program_id(1)
    n_blocks = k_ref.shape[0] // MOBA_BLOCK

    @pl.when(i == 0)
    def _():
        km_ref[...] = jnp.zeros_like(km_ref)

        def body(b, carry):
            kb = k_ref[pl.ds(pl.multiple_of(b * MOBA_BLOCK, MOBA_BLOCK), MOBA_BLOCK), :].astype(F32)
            km_ref[pl.ds(b, 1), :] = jnp.sum(kb, axis=0, keepdims=True) * (1.0 / MOBA_BLOCK)
            return carry

        lax.fori_loop(0, n_blocks, body, 0)

    q = q_ref[...].astype(F32)
    kt = k_ref[pl.ds(pl.multiple_of(i * ATT_TILE, ATT_TILE), ATT_TILE), :].astype(F32)
    lane_i = lax.broadcasted_iota(jnp.int32, q.shape, 1)
    lane = lane_i.astype(F32)
    low = lane_i < HEAD_DIM
    blk = i.astype(F32)
    km = km_ref[...]
    onehot = jnp.where(lane_i - HEAD_DIM == i, 1.0, 0.0)
    q_sw = pltpu.roll(q, HEAD_DIM, 1)
    k_sw = pltpu.roll(kt, HEAD_DIM, 1)
    for h in range(2):
        qh = jnp.where(low if h == 0 else jnp.logical_not(low), q, 0.0)
        gate = lax.dot_general(qh, km, _CONTRACT_LANES, precision=lax.Precision.HIGHEST,
                               preferred_element_type=F32)
        g = jnp.where(lane < blk, gate, -jnp.inf)
        sel = lane == blk
        for _ in range(MOBA_TOPK):
            m = jnp.max(g, axis=1, keepdims=True)
            first = jnp.min(jnp.where(g == m, lane, float(LANES)), axis=1, keepdims=True)
            pick = jnp.logical_and(lane == first, m > -jnp.inf)
            sel = jnp.logical_or(sel, pick)
            g = jnp.where(pick, -jnp.inf, g)
        bias = pltpu.roll(jnp.where(sel, 0.0, NEG), HEAD_DIM, 1)
        qa_ref[h] = jnp.where(low, q if h == 0 else q_sw, bias).astype(BF16)
        ka_ref[h] = jnp.where(low, kt if h == 0 else k_sw, onehot).astype(BF16)


def _moba_flash_kernel(qa_ref, ka_ref, vt_ref, o_ref, acc_ref, s_ref, p_ref):
    i = pl.program_id(1)
    key = lax.broadcasted_iota(jnp.int32, (ATT_TILE, ATT_TILE), 0)
    qry = lax.broadcasted_iota(jnp.int32, (ATT_TILE, ATT_TILE), 1)

    def scores(j, h):
        off = pl.multiple_of(j * ATT_TILE, ATT_TILE)
        return lax.dot_general(ka_ref[h, pl.ds(off, ATT_TILE), :], qa_ref[h], _CONTRACT_LANES,
                               preferred_element_type=F32)

    def logits(j, h, s_t, is_last):
        return jnp.where(key <= qry, s_t, -jnp.inf) if is_last else s_t

    def values(j, h):
        return vt_ref[j, h * HEAD_DIM:(h + 1) * HEAD_DIM, :]

    o_t = _attention_sweep(0, i, scores, logits, values, acc_ref, s_ref, p_ref)
    o_ref[...] = o_t.T.astype(BF16)


def _moba_attention(qk, vt):
    S = qk.shape[0]
    n_pairs = MOBA_HEADS // 2
    k_col0 = D_MODEL // LANES
    aug = jax.ShapeDtypeStruct((MOBA_HEADS, S, LANES), BF16)
    qa, ka = pl.pallas_call(
        _moba_select_kernel,
        grid=(n_pairs, S // ATT_TILE),
        in_specs=[pl.BlockSpec((ATT_TILE, LANES), lambda p, i: (i, p)),
                  pl.BlockSpec((S, LANES), lambda p, i: (0, k_col0 + p))],
        out_specs=[pl.BlockSpec((2, ATT_TILE, LANES), lambda p, i: (p, i, 0))] * 2,
        out_shape=[aug, aug],
        scratch_shapes=[pltpu.VMEM((LANES, LANES), F32)],
        compiler_params=_params("parallel", "arbitrary"),
        name="moba_select",
    )(qk, qk)
    return pl.pallas_call(
        _moba_flash_kernel,
        grid=(n_pairs, S // ATT_TILE),
        in_specs=[pl.BlockSpec((2, ATT_TILE, LANES), lambda p, i: (p, i, 0)),
                  pl.BlockSpec((2, S, LANES), lambda p, i: (p, 0, 0)),
                  pl.BlockSpec((S // ATT_TILE, LANES, ATT_TILE), lambda p, i: (0, p, 0))],
        out_specs=pl.BlockSpec((ATT_TILE, LANES), lambda p, i: (i, p)),
        out_shape=jax.ShapeDtypeStruct((S, n_pairs * LANES), BF16),
        scratch_shapes=[pltpu.VMEM((LANES, ATT_TILE), F32),
                        pltpu.VMEM((2, 2, ATT_TILE, ATT_TILE), F32),
                        pltpu.VMEM((2, 2, ATT_TILE, ATT_TILE), BF16)],
        compiler_params=_params("parallel", "parallel"),
        name="moba_flash",
    )(qa, ka, vt)


def _dilated_bias_table():
    key = np.arange(ATT_TILE)[:, None]
    qry = np.arange(ATT_TILE)[None, :]
    tiles = []
    for t in range(DILATED_KEY_TILES):
        d = qry - key + ATT_TILE * (DILATED_KEY_TILES - 1 - t)
        count = np.zeros_like(d)
        for window, dil in DILATED_PAIRS:
            count += ((d >= 0) & (d <= window) & (d % dil == 0)).astype(d.dtype)
        tiles.append(np.where(count > 0, np.log2(np.maximum(count, 1)), NEG))
    tiles.append(np.full_like(tiles[0], NEG))
    return jnp.asarray(np.stack(tiles), dtype=F32)


def _dilated_kernel(q_ref, k_ref, vt_ref, tab_ref, o_ref, acc_ref, s_ref, p_ref, qh_ref):
    i = pl.program_id(1)
    q = q_ref[...]
    low = lax.broadcasted_iota(jnp.int32, q.shape, 1) < HEAD_DIM
    zero = jnp.zeros_like(q)
    qh_ref[0] = jnp.where(low, q, zero)
    qh_ref[1] = jnp.where(low, zero, q)
    last = DILATED_KEY_TILES - 1

    def tile(t):
        return jnp.maximum(i - last + t, 0)

    def scores(t, h):
        off = pl.multiple_of(tile(t) * ATT_TILE, ATT_TILE)
        return lax.dot_general(k_ref[pl.ds(off, ATT_TILE), :], qh_ref[h], _CONTRACT_LANES,
                               preferred_element_type=F32)

    def logits(t, h, s_t, is_last):
        return s_t + tab_ref[jnp.where(i - last + t < 0, DILATED_KEY_TILES, t)]

    def values(t, h):
        return vt_ref[tile(t), h * HEAD_DIM:(h + 1) * HEAD_DIM, :]

    o_t = _attention_sweep(0, last, scores, logits, values, acc_ref, s_ref, p_ref)
    o_ref[...] = o_t.T.astype(BF16)


def _dilated_attention(qk, vt):
    S = qk.shape[0]
    n_pairs = (N_HEADS - MOBA_HEADS) // 2
    q_col0 = MOBA_HEADS // 2
    k_col0 = D_MODEL // LANES + q_col0
    return pl.pallas_call(
        _dilated_kernel,
        grid=(n_pairs, S // ATT_TILE),
        in_specs=[pl.BlockSpec((ATT_TILE, LANES), lambda p, i: (i, q_col0 + p)),
                  pl.BlockSpec((S, LANES), lambda p, i: (0, k_col0 + p)),
                  pl.BlockSpec((S // ATT_TILE, LANES, ATT_TILE), lambda p, i: (0, q_col0 + p, 0)),
                  pl.BlockSpec((DILATED_KEY_TILES + 1, ATT_TILE, ATT_TILE), lambda p, i: (0, 0, 0))],
        out_specs=pl.BlockSpec((ATT_TILE, LANES), lambda p, i: (i, p)),
        out_shape=jax.ShapeDtypeStruct((S, n_pairs * LANES), BF16),
        scratch_shapes=[pltpu.VMEM((LANES, ATT_TILE), F32),
                        pltpu.VMEM((2, 2, ATT_TILE, ATT_TILE), F32),
                        pltpu.VMEM((2, 2, ATT_TILE, ATT_TILE), BF16),
                        pltpu.VMEM((2, ATT_TILE, LANES), BF16)],
        compiler_params=_params("parallel", "parallel"),
        name="dilated_attention",
    )(qk, qk, vt, _dilated_bias_table())


SB_HEADS_PER_STEP = 2


def _sb_kernel(q_ref, k_ref, vt_ref, o_ref, acc_ref):
    i = pl.program_id(1)
    heads = range(SB_HEADS_PER_STEP)
    key = lax.broadcasted_iota(jnp.int32, (ATT_TILE, ATT_TILE), 0)
    qry = lax.broadcasted_iota(jnp.int32, (ATT_TILE, ATT_TILE), 1)
    suffix = jnp.where(qry >= key, 1.0, 0.0).astype(BF16)
    acc_ref[...] = jnp.zeros_like(acc_ref)
    valid = key < qry

    def lanes(h):
        return slice(h * LANES, (h + 1) * LANES)

    def front(j, h, diagonal):
        k = k_ref[pl.ds(pl.multiple_of(j * ATT_TILE, ATT_TILE), ATT_TILE), lanes(h)]
        z = lax.dot_general(k, q_ref[:, lanes(h)], _CONTRACT_LANES, preferred_element_type=F32)
        sp = jnp.maximum(z, 0.0) + jnp.log(1.0 + jnp.exp(-jnp.abs(z)))
        if diagonal:
            sp = jnp.where(valid, sp, 0.0)
        sp_hi = sp.astype(BF16)
        sp_lo = (sp - sp_hi.astype(F32)).astype(BF16)
        r_in = (jnp.dot(suffix, sp_hi, preferred_element_type=F32)
                + jnp.dot(suffix, sp_lo, preferred_element_type=F32))
        return z, r_in

    def back(j, h, z, r_in, off, keep):
        a = jnp.where(keep, jnp.exp(z - (r_in + off)), 0.0)
        acc_ref[lanes(h), :] += jnp.dot(vt_ref[j, lanes(h), :], a.astype(BF16), preferred_element_type=F32)

    has_prev = i > 0
    j_prev = jnp.maximum(i - 1, 0)
    own = [front(i, h, True) for h in heads]
    prev = [front(j_prev, h, False) for h in heads]
    offs = []
    for h in heads:
        (z0, r0), (z1, r1) = own[h], prev[h]
        back(i, h, z0, r0, jnp.zeros((1, ATT_TILE), F32), valid)
        off = r0[0:1, :]
        back(j_prev, h, z1, r1, off, has_prev)
        offs.append(off + jnp.where(has_prev, r1[0:1, :], 0.0))
    offs = tuple(offs)

    def least(offs):
        return functools.reduce(jnp.minimum, [jnp.min(off) for off in offs])

    def cond(carry):
        n, _, min_off = carry
        return jnp.logical_and(n < i, min_off <= SB_UNDERFLOW)

    def body(carry):
        n, offs, _ = carry
        j = i - 1 - n
        new = []
        for h in heads:
            z, r_in = front(j, h, False)
            back(j, h, z, r_in, offs[h], True)
            new.append(offs[h] + r_in[0:1, :])
        return n + 1, tuple(new), least(new)

    lax.while_loop(cond, body, (jnp.int32(1), offs, least(offs)))
    o_ref[...] = acc_ref[...].T.astype(BF16)


def _stick_breaking_attention(qk, vt):
    S = qk.shape[0]
    width = SB_HEADS_PER_STEP * LANES
    k_col0 = D_MODEL // width
    return pl.pallas_call(
        _sb_kernel,
        grid=(SB_HEADS // SB_HEADS_PER_STEP, S // ATT_TILE),
        in_specs=[pl.BlockSpec((ATT_TILE, width), lambda g, i: (i, g)),
                  pl.BlockSpec((S, width), lambda g, i: (0, k_col0 + g)),
                  pl.BlockSpec((S // ATT_TILE, width, ATT_TILE), lambda g, i: (0, g, 0))],
        out_specs=pl.BlockSpec((ATT_TILE, width), lambda g, i: (i, g)),
        out_shape=jax.ShapeDtypeStruct((S, D_MODEL), BF16),
        scratch_shapes=[pltpu.VMEM((width, ATT_TILE), F32)],
        compiler_params=_params("parallel", "parallel"),
        name="stick_breaking",
    )(qk, qk, vt)


def _proj_ln_kernel(*refs, n_parts):
    o_refs = refs[:n_parts]
    w_refs = refs[n_parts:2 * n_parts]
    x_ref, g_ref, b_ref, out_ref = refs[2 * n_parts:]
    m = jnp.dot(o_refs[0][...], w_refs[0][...], preferred_element_type=F32)
    for o_ref, w_ref in zip(o_refs[1:], w_refs[1:]):
        m = m + jnp.dot(o_ref[...], w_ref[...], preferred_element_type=F32)
    out_ref[...] = _layer_norm(DEEPNORM_ALPHA * x_ref[...] + m, g_ref[...], b_ref[...])


def _proj_ln(o_parts, w_parts, x, g, b):
    S = x.shape[0]
    n = len(o_parts)
    in_specs = ([pl.BlockSpec((ROW_TILE, o.shape[1]), lambda i: (i, 0)) for o in o_parts]
                + [pl.BlockSpec(w.shape, lambda i: (0, 0)) for w in w_parts]
                + [pl.BlockSpec((ROW_TILE, D_MODEL), lambda i: (i, 0)),
                   pl.BlockSpec((1, D_MODEL), lambda i: (0, 0)),
                   pl.BlockSpec((1, D_MODEL), lambda i: (0, 0))])
    return pl.pallas_call(
        functools.partial(_proj_ln_kernel, n_parts=n),
        grid=(S // ROW_TILE,),
        in_specs=in_specs,
        out_specs=pl.BlockSpec((ROW_TILE, D_MODEL), lambda i: (i, 0)),
        out_shape=jax.ShapeDtypeStruct((S, D_MODEL), F32),
        compiler_params=_params("parallel"),
        name="out_proj_ln",
    )(*o_parts, *w_parts, x, g.reshape(1, D_MODEL), b.reshape(1, D_MODEL))


def _ffn_kernel(x_ref, wup_ref, cw_ref, cb_ref, wd_ref, g_ref, b_ref, out_ref,
                xb_ref, acc_ref, u_ref, h_ref, carry_ref):
    i = pl.program_id(0)
    rows = x_ref.shape[0]
    tf = h_ref.shape[-1]
    n_f = FFN_DIM // tf

    def cols(part, f):
        return slice(part * FFN_DIM + f * tf, part * FFN_DIM + (f + 1) * tf)

    @pl.when(i == 0)
    def _():
        carry_ref[...] = jnp.zeros_like(carry_ref)

    xb_ref[...] = x_ref[...].astype(BF16)
    acc_ref[...] = jnp.zeros_like(acc_ref)
    h_ref[...] = jnp.zeros_like(h_ref)

    def up(f):
        xb = xb_ref[...]
        return [jnp.dot(xb, wup_ref[:, cols(part, f)], preferred_element_type=F32) for part in range(2)]

    def load_u(f, slot, u):
        for part in range(2):
            u_ref[slot, part, 0:SUBLANES, :] = carry_ref[f, part]
            u_ref[slot, part, SUBLANES:SUBLANES + rows, :] = u[part]

    def down(f, slot):
        acc_ref[...] += jnp.dot(h_ref[slot], wd_ref[f * tf:(f + 1) * tf, :], preferred_element_type=F32)

    def gate(f, slot):
        c = []
        for part in range(2):
            cw = cw_ref[:, cols(part, f)]
            acc = cb_ref[:, cols(part, f)]
            for tap in range(CONV_WIDTH):
                start = SUBLANES - (CONV_WIDTH - 1 - tap)
                acc = acc + cw[tap:tap + 1, :] * u_ref[slot, part, start:start + rows, :]
            c.append(acc)
            carry_ref[f, part] = u_ref[slot, part, rows:rows + SUBLANES, :]
        h_ref[slot] = (c[0] * (1.0 / (1.0 + jnp.exp(-c[0]))) * c[1]).astype(BF16)

    def trip(f, slot):
        u_next = up(f + 1)
        down(max(f - 1, 0), 1 - slot)
        gate(f, slot)
        load_u(f + 1, 1 - slot, u_next)

    load_u(0, 0, up(0))
    for f in range(n_f - 1):
        trip(f, f % 2)
    last = n_f - 1
    down(last - 1, 1 - last % 2)
    gate(last, last % 2)
    down(last, last % 2)
    out_ref[...] = _layer_norm(DEEPNORM_ALPHA * x_ref[...] + acc_ref[...], g_ref[...], b_ref[...])


def _conv_ffn_ln(x, w_up, conv_w, conv_b, w_down, g, b):
    S = x.shape[0]
    tm, tf = FFN_ROW_TILE, FFN_COL_TILE
    n_f = FFN_DIM // tf
    whole = lambda a: pl.BlockSpec(a.shape, lambda i: (0,) * a.ndim, pipeline_mode=pl.Buffered(1))
    g2, b2, cb2 = g.reshape(1, D_MODEL), b.reshape(1, D_MODEL), conv_b.reshape(1, 2 * FFN_DIM)
    w_up_b, w_down_b = w_up.astype(BF16), w_down.astype(BF16)
    return pl.pallas_call(
        _ffn_kernel,
        grid=(S // tm,),
        in_specs=[pl.BlockSpec((tm, D_MODEL), lambda i: (i, 0)),
                  whole(w_up_b), whole(conv_w), whole(cb2), whole(w_down_b), whole(g2), whole(b2)],
        out_specs=pl.BlockSpec((tm, D_MODEL), lambda i: (i, 0)),
        out_shape=jax.ShapeDtypeStruct((S, D_MODEL), F32),
        scratch_shapes=[pltpu.VMEM((tm, D_MODEL), BF16),
                        pltpu.VMEM((tm, D_MODEL), F32),
                        pltpu.VMEM((2, 2, tm + 2 * SUBLANES, tf), F32),
                        pltpu.VMEM((2, tm, tf), BF16),
                        pltpu.VMEM((n_f, 2, SUBLANES, tf), F32)],
        compiler_params=_params("arbitrary"),
        name="conv_ffn_ln",
    )(x, w_up_b, conv_w, cb2, w_down_b, g2, b2)


def kernel(x, w_qkv_ab, w_o_ab, w_qkv_c, w_o_c, ln_mix_g, ln_mix_b, w_up, conv_w, conv_b, w_down,
           ln_ffn_g, ln_ffn_b):
    batch, S, _ = x.shape
    assert S % ROW_TILE == 0 and S % FFN_ROW_TILE == 0 and S // MOBA_BLOCK <= HEAD_DIM
    rope_tables = _rope_tables(S)
    moba_width = MOBA_HEADS * HEAD_DIM
    outs = []
    for bi in range(batch):
        h = x[bi]
        for layer in range(DEPTH):
            if layer % 2 == 0:
                qk, vt = _qkv_proj(h, w_qkv_ab[layer // 2], HEAD_DIM ** -0.5 * LOG2_E, rope_tables)
                w_o = w_o_ab[layer // 2].astype(BF16)
                o_parts = [_moba_attention(qk, vt), _dilated_attention(qk, vt)]
                w_parts = [w_o[:moba_width], w_o[moba_width:]]
            else:
                qk, vt = _qkv_proj(h, w_qkv_c[layer // 2], SB_HEAD_DIM ** -0.5)
                o_parts = [_stick_breaking_attention(qk, vt)]
                w_parts = [w_o_c[layer // 2].astype(BF16)]
            h = _proj_ln(o_parts, w_parts, h, ln_mix_g[layer], ln_mix_b[layer])
            h = _conv_ffn_ln(h, w_up[layer], conv_w[layer], conv_b[layer], w_down[layer],
                             ln_ffn_g[layer], ln_ffn_b[layer])
        outs.append(h)
    return jnp.stack(outs)
```

```python
import functools

import numpy as np
import jax
import jax.numpy as jnp
from jax import lax
from jax.experimental import pallas as pl
from jax.experimental.pallas import tpu as pltpu

D_MODEL = 1024
DEPTH = 4
HEAD_DIM = 64
N_HEADS = D_MODEL // HEAD_DIM
MOBA_HEADS = N_HEADS // 4
SB_HEAD_DIM = 128
SB_HEADS = D_MODEL // SB_HEAD_DIM
MOBA_BLOCK = 256
MOBA_TOPK = 3
DILATED_PAIRS = ((128, 1), (512, 4), (2048, 16))
FFN_DIM = ((8 * D_MODEL // 3 + 127) // 128) * 128
CONV_WIDTH = 3
ROPE_THETA = 10000.0
LN_EPS = 1e-5
DEEPNORM_ALPHA = (2 * DEPTH) ** 0.25

LANES = 128
SUBLANES = 8
VMEM_LIMIT_BYTES = 56 * 1024 * 1024

ROW_TILE = 512
FFN_ROW_TILE = 512
FFN_COL_TILE = 256
ATT_TILE = 256
ACC_ROWS = HEAD_DIM + 16
MAX_DILATED_WINDOW = max(w for w, _ in DILATED_PAIRS)
DILATED_KEY_TILES = MAX_DILATED_WINDOW // ATT_TILE + 1

NEG = -1e30
LOG2_E = float(np.log2(np.e))
SB_UNDERFLOW = 104.0
F32 = jnp.float32
BF16 = jnp.bfloat16

_CONTRACT_LANES = (((1,), (1,)), ((), ()))


def _params(*semantics, flags=None):
    return pltpu.CompilerParams(dimension_semantics=semantics, vmem_limit_bytes=VMEM_LIMIT_BYTES,
                                flags=flags)


def _layer_norm(y, g, b):
    mu = jnp.mean(y, axis=-1, keepdims=True)
    yc = y - mu
    var = jnp.mean(yc * yc, axis=-1, keepdims=True)
    return yc * lax.rsqrt(var + LN_EPS) * g + b


def _qkv_kernel(*refs, rope, q_scale):
    if rope:
        x_ref, w_ref, wvt_ref, cos_ref, sin_ref, qk_ref, vt_ref = refs
    else:
        x_ref, w_ref, wvt_ref, qk_ref, vt_ref = refs
    xb = x_ref[...].astype(BF16)
    for part in range(2):
        y = jnp.dot(xb, w_ref[:, part * D_MODEL:(part + 1) * D_MODEL], preferred_element_type=F32)
        if rope:
            c = cos_ref[...]
            s = sin_ref[...]
            lane = lax.broadcasted_iota(jnp.int32, c.shape, 1)
            first_half = (lane % HEAD_DIM) < (HEAD_DIM // 2)
            for cb in range(D_MODEL // LANES):
                yc = y[:, cb * LANES:(cb + 1) * LANES]
                partner = jnp.where(first_half,
                                    pltpu.roll(yc, LANES - HEAD_DIM // 2, 1),
                                    pltpu.roll(yc, HEAD_DIM // 2, 1))
                yc = yc * c + partner * s
                if part == 0:
                    yc = yc * q_scale
                qk_ref[:, part * D_MODEL + cb * LANES:part * D_MODEL + (cb + 1) * LANES] = yc.astype(BF16)
        else:
            if part == 0:
                y = y * q_scale
            qk_ref[:, part * D_MODEL:(part + 1) * D_MODEL] = y.astype(BF16)
    vt = lax.dot_general(wvt_ref[...], xb, _CONTRACT_LANES,
                         preferred_element_type=F32).astype(BF16)
    for t in range(ROW_TILE // ATT_TILE):
        vt_ref[t] = vt[:, t * ATT_TILE:(t + 1) * ATT_TILE]


def _qkv_proj(x, w_qkv, q_scale, rope_tables=None):
    w_qk = w_qkv[:, :2 * D_MODEL].astype(BF16)
    w_vt = w_qkv[:, 2 * D_MODEL:].T.astype(BF16)
    S = x.shape[0]
    rope = rope_tables is not None
    tiles = ROW_TILE // ATT_TILE
    in_specs = [pl.BlockSpec((ROW_TILE, D_MODEL), lambda i: (i, 0)),
                pl.BlockSpec((D_MODEL, 2 * D_MODEL), lambda i: (0, 0)),
                pl.BlockSpec((D_MODEL, D_MODEL), lambda i: (0, 0))]
    args = [x, w_qk, w_vt]
    if rope:
        in_specs += [pl.BlockSpec((ROW_TILE, LANES), lambda i: (i, 0))] * 2
        args += list(rope_tables)
    return pl.pallas_call(
        functools.partial(_qkv_kernel, rope=rope, q_scale=q_scale),
        grid=(S // ROW_TILE,),
        in_specs=in_specs,
        out_specs=[pl.BlockSpec((ROW_TILE, 2 * D_MODEL), lambda i: (i, 0)),
                   pl.BlockSpec((tiles, D_MODEL, ATT_TILE), lambda i: (i, 0, 0))],
        out_shape=[jax.ShapeDtypeStruct((S, 2 * D_MODEL), BF16),
                   jax.ShapeDtypeStruct((S // ATT_TILE, D_MODEL, ATT_TILE), BF16)],
        compiler_params=_params("parallel"),
        name="qkv_proj",
    )(*args)


def _rope_tables(S):
    inv = 1.0 / (ROPE_THETA ** (jnp.arange(0, HEAD_DIM, 2, dtype=F32) / HEAD_DIM))
    ang = jnp.arange(S, dtype=F32)[:, None] * inv[None, :]
    cos, sin = jnp.cos(ang), jnp.sin(ang)
    cos_t = jnp.tile(jnp.concatenate([cos, cos], axis=-1), (1, LANES // HEAD_DIM))
    sin_t = jnp.tile(jnp.concatenate([-sin, sin], axis=-1), (1, LANES // HEAD_DIM))
    return cos_t, sin_t


def _softmax_step(s_t, m_old):
    m_new = jnp.maximum(m_old, jnp.max(s_t, axis=0, keepdims=True))
    p_t = jnp.exp2(s_t - m_new)
    alpha = jnp.exp2(m_old - m_new)
    return p_t.astype(BF16), alpha, m_new


def _attention_sweep(first, last, scores, logits, values, acc_ref, s_ref, p_ref):
    acc_ref[...] = jnp.zeros_like(acc_ref)
    p_ref[...] = jnp.zeros_like(p_ref)
    stat0 = jnp.full((1, ATT_TILE), -jnp.inf, F32)
    one = jnp.ones((1, ATT_TILE), F32)
    ones_rows = jnp.ones((ACC_ROWS - HEAD_DIM, ATT_TILE), BF16)
    unrolled = isinstance(first, int) and isinstance(last, int)

    def accumulate(t, slot, h, alpha):
        rows = slice(h * ACC_ROWS, (h + 1) * ACC_ROWS)
        lhs = jnp.concatenate([values(t, h), ones_rows], axis=0)
        acc_ref[rows, :] = alpha * acc_ref[rows, :] + jnp.dot(lhs, p_ref[slot, h],
                                                              preferred_element_type=F32)

    def softmax(t, slot, h, stat, is_last):
        p_t, alpha, m_new = _softmax_step(logits(t, h, s_ref[slot, h], is_last), stat)
        p_ref[slot, h] = p_t
        return alpha, m_new

    def trip(t, slot, carry):
        stats, alphas = carry
        prev = max(t - 1, first) if unrolled else jnp.maximum(t - 1, first)
        s_next = [scores(t + 1, h) for h in range(2)]
        new_stats, new_alphas = [], []
        for h in range(2):
            accumulate(prev, 1 - slot, h, alphas[h])
            alpha, stat = softmax(t, slot, h, stats[h], False)
            s_ref[1 - slot, h] = s_next[h]
            new_stats.append(stat)
            new_alphas.append(alpha)
        return tuple(new_stats), tuple(new_alphas)

    def finish(slot, carry):
        stats, alphas = carry
        prev = max(last - 1, first) if unrolled else jnp.maximum(last - 1, first)
        inv = []
        for h in range(2):
            accumulate(prev, 1 - slot, h, alphas[h])
            alpha, _ = softmax(last, slot, h, stats[h], True)
            accumulate(last, slot, h, alpha)
            inv.append(1.0 / acc_ref[h * ACC_ROWS + HEAD_DIM:h * ACC_ROWS + HEAD_DIM + 1, :])
        return tuple(inv)

    for h in range(2):
        s_ref[0, h] = scores(first, h)
    carry = ((stat0, stat0), (one, one))
    if unrolled:
        for t in range(first, last):
            carry = trip(t, (t - first) % 2, carry)
        inv = finish((last - first) % 2, carry)
    else:
        n = last - first

        def quad(k, carry):
            t = first + 4 * k
            for d in range(4):
                carry = trip(t + d, d % 2, carry)
            return carry

        def pair(k, carry):
            t = first + (n // 4) * 4 + 2 * k
            return trip(t + 1, 1, trip(t, 0, carry))

        carry = lax.fori_loop(0, n // 4, quad, carry)
        carry = lax.fori_loop(0, (n % 4) // 2, pair, carry)
        inv = lax.cond(n % 2 == 1,
                       lambda c: finish(1, trip(last - 1, 0, c)),
                       lambda c: finish(0, c), carry)
    return jnp.concatenate([acc_ref[h * ACC_ROWS:h * ACC_ROWS + HEAD_DIM, :] * inv[h] for h in range(2)], axis=0)


def _moba_select_kernel(q_ref, k_ref, qa_ref, ka_ref, km_ref):
    i = pl.program_id(1)
    n_blocks = k_ref.shape[0] // MOBA_BLOCK

    @pl.when(i == 0)
    def _():
        km_ref[...] = jnp.zeros_like(km_ref)

        def body(b, carry):
            kb = k_ref[pl.ds(pl.multiple_of(b * MOBA_BLOCK, MOBA_BLOCK), MOBA_BLOCK), :].astype(F32)
            km_ref[pl.ds(b, 1), :] = jnp.sum(kb, axis=0, keepdims=True) * (1.0 / MOBA_BLOCK)
            return carry

        lax.fori_loop(0, n_blocks, body, 0)

    q = q_ref[...].astype(F32)
    kt = k_ref[pl.ds(pl.multiple_of(i * ATT_TILE, ATT_TILE), ATT_TILE), :].astype(F32)
    lane_i = lax.broadcasted_iota(jnp.int32, q.shape, 1)
    lane = lane_i.astype(F32)
    low = lane_i < HEAD_DIM
    blk = i.astype(F32)
    km = km_ref[...]
    onehot = jnp.where(lane_i - HEAD_DIM == i, 1.0, 0.0)
    q_sw = pltpu.roll(q, HEAD_DIM, 1)
    k_sw = pltpu.roll(kt, HEAD_DIM, 1)
    for h in range(2):
        qh = jnp.where(low if h == 0 else jnp.logical_not(low), q, 0.0)
        gate = lax.dot_general(qh, km, _CONTRACT_LANES, precision=lax.Precision.HIGHEST,
                               preferred_element_type=F32)
        g = jnp.where(lane < blk, gate, -jnp.inf)
        sel = lane == blk
        for _ in range(MOBA_TOPK):
            m = jnp.max(g, axis=1, keepdims=True)
            first = jnp.min(jnp.where(g == m, lane, float(LANES)), axis=1, keepdims=True)
            pick = jnp.logical_and(lane == first, m > -jnp.inf)
            sel = jnp.logical_or(sel, pick)
            g = jnp.where(pick, -jnp.inf, g)
        bias = pltpu.roll(jnp.where(sel, 0.0, NEG), HEAD_DIM, 1)
        qa_ref[h] = jnp.where(low, q if h == 0 else q_sw, bias).astype(BF16)
        ka_ref[h] = jnp.where(low, kt if h == 0 else k_sw, onehot).astype(BF16)


def _moba_flash_kernel(qa_ref, ka_ref, vt_ref, o_ref, acc_ref, s_ref, p_ref):
    i = pl.program_id(1)
    key = lax.broadcasted_iota(jnp.int32, (ATT_TILE, ATT_TILE), 0)
    qry = lax.broadcasted_iota(jnp.int32, (ATT_TILE, ATT_TILE), 1)

    def scores(j, h):
        off = pl.multiple_of(j * ATT_TILE, ATT_TILE)
        return lax.dot_general(ka_ref[h, pl.ds(off, ATT_TILE), :], qa_ref[h], _CONTRACT_LANES,
                               preferred_element_type=F32)

    def logits(j, h, s_t, is_last):
        return jnp.where(key <= qry, s_t, -jnp.inf) if is_last else s_t

    def values(j, h):
        return vt_ref[j, h * HEAD_DIM:(h + 1) * HEAD_DIM, :]

    o_t = _attention_sweep(0, i, scores, logits, values, acc_ref, s_ref, p_ref)
    o_ref[...] = o_t.T.astype(BF16)


def _moba_attention(qk, vt):
    S = qk.shape[0]
    n_pairs = MOBA_HEADS // 2
    k_col0 = D_MODEL // LANES
    aug = jax.ShapeDtypeStruct((MOBA_HEADS, S, LANES), BF16)
    qa, ka = pl.pallas_call(
        _moba_select_kernel,
        grid=(n_pairs, S // ATT_TILE),
        in_specs=[pl.BlockSpec((ATT_TILE, LANES), lambda p, i: (i, p)),
                  pl.BlockSpec((S, LANES), lambda p, i: (0, k_col0 + p))],
        out_specs=[pl.BlockSpec((2, ATT_TILE, LANES), lambda p, i: (p, i, 0))] * 2,
        out_shape=[aug, aug],
        scratch_shapes=[pltpu.VMEM((LANES, LANES), F32)],
        compiler_params=_params("parallel", "arbitrary"),
        name="moba_select",
    )(qk, qk)
    return pl.pallas_call(
        _moba_flash_kernel,
        grid=(n_pairs, S // ATT_TILE),
        in_specs=[pl.BlockSpec((2, ATT_TILE, LANES), lambda p, i: (p, i, 0)),
                  pl.BlockSpec((2, S, LANES), lambda p, i: (p, 0, 0)),
                  pl.BlockSpec((S // ATT_TILE, LANES, ATT_TILE), lambda p, i: (0, p, 0))],
        out_specs=pl.BlockSpec((ATT_TILE, LANES), lambda p, i: (i, p)),
        out_shape=jax.ShapeDtypeStruct((S, n_pairs * LANES), BF16),
        scratch_shapes=[pltpu.VMEM((2 * ACC_ROWS, ATT_TILE), F32),
                        pltpu.VMEM((2, 2, ATT_TILE, ATT_TILE), F32),
                        pltpu.VMEM((2, 2, ATT_TILE, ATT_TILE), BF16)],
        compiler_params=_params("parallel", "parallel"),
        name="moba_flash",
    )(qa, ka, vt)


def _dilated_bias_table():
    key = np.arange(ATT_TILE)[:, None]
    qry = np.arange(ATT_TILE)[None, :]
    tiles = []
    for t in range(DILATED_KEY_TILES):
        d = qry - key + ATT_TILE * (DILATED_KEY_TILES - 1 - t)
        count = np.zeros_like(d)
        for window, dil in DILATED_PAIRS:
            count += ((d >= 0) & (d <= window) & (d % dil == 0)).astype(d.dtype)
        tiles.append(np.where(count > 0, np.log2(np.maximum(count, 1)), NEG))
    tiles.append(np.full_like(tiles[0], NEG))
    return jnp.asarray(np.stack(tiles), dtype=F32)


def _dilated_kernel(q_ref, k_ref, vt_ref, tab_ref, o_ref, acc_ref, s_ref, p_ref, qh_ref):
    i = pl.program_id(1)
    q = q_ref[...]
    low = lax.broadcasted_iota(jnp.int32, q.shape, 1) < HEAD_DIM
    zero = jnp.zeros_like(q)
    qh_ref[0] = jnp.where(low, q, zero)
    qh_ref[1] = jnp.where(low, zero, q)
    last = DILATED_KEY_TILES - 1

    def tile(t):
        return jnp.maximum(i - last + t, 0)

    def scores(t, h):
        off = pl.multiple_of(tile(t) * ATT_TILE, ATT_TILE)
        return lax.dot_general(k_ref[pl.ds(off, ATT_TILE), :], qh_ref[h], _CONTRACT_LANES,
                               preferred_element_type=F32)

    def logits(t, h, s_t, is_last):
        return s_t + tab_ref[jnp.where(i - last + t < 0, DILATED_KEY_TILES, t)]

    def values(t, h):
        return vt_ref[tile(t), h * HEAD_DIM:(h + 1) * HEAD_DIM, :]

    o_t = _attention_sweep(0, last, scores, logits, values, acc_ref, s_ref, p_ref)
    o_ref[...] = o_t.T.astype(BF16)


def _dilated_attention(qk, vt):
    S = qk.shape[0]
    n_pairs = (N_HEADS - MOBA_HEADS) // 2
    q_col0 = MOBA_HEADS // 2
    k_col0 = D_MODEL // LANES + q_col0
    return pl.pallas_call(
        _dilated_kernel,
        grid=(n_pairs, S // ATT_TILE),
        in_specs=[pl.BlockSpec((ATT_TILE, LANES), lambda p, i: (i, q_col0 + p)),
                  pl.BlockSpec((S, LANES), lambda p, i: (0, k_col0 + p)),
                  pl.BlockSpec((S // ATT_TILE, LANES, ATT_TILE), lambda p, i: (0, q_col0 + p, 0)),
                  pl.BlockSpec((DILATED_KEY_TILES + 1, ATT_TILE, ATT_TILE), lambda p, i: (0, 0, 0))],
        out_specs=pl.BlockSpec((ATT_TILE, LANES), lambda p, i: (i, p)),
        out_shape=jax.ShapeDtypeStruct((S, n_pairs * LANES), BF16),
        scratch_shapes=[pltpu.VMEM((2 * ACC_ROWS, ATT_TILE), F32),
                        pltpu.VMEM((2, 2, ATT_TILE, ATT_TILE), F32),
                        pltpu.VMEM((2, 2, ATT_TILE, ATT_TILE), BF16),
                        pltpu.VMEM((2, ATT_TILE, LANES), BF16)],
        compiler_params=_params("parallel", "parallel"),
        name="dilated_attention",
    )(qk, qk, vt, _dilated_bias_table())


SB_HEADS_PER_STEP = 2


def _sb_kernel(q_ref, k_ref, vt_ref, o_ref, acc_ref):
    i = pl.program_id(1)
    heads = range(SB_HEADS_PER_STEP)
    key = lax.broadcasted_iota(jnp.int32, (ATT_TILE, ATT_TILE), 0)
    qry = lax.broadcasted_iota(jnp.int32, (ATT_TILE, ATT_TILE), 1)
    suffix = jnp.where(qry >= key, 1.0, 0.0).astype(BF16)
    acc_ref[...] = jnp.zeros_like(acc_ref)
    valid = key < qry

    def lanes(h):
        return slice(h * LANES, (h + 1) * LANES)

    def front(j, h, diagonal):
        k = k_ref[pl.ds(pl.multiple_of(j * ATT_TILE, ATT_TILE), ATT_TILE), lanes(h)]
        z = lax.dot_general(k, q_ref[:, lanes(h)], _CONTRACT_LANES, preferred_element_type=F32)
        sp = jnp.maximum(z, 0.0) + jnp.log(1.0 + jnp.exp(-jnp.abs(z)))
        if diagonal:
            sp = jnp.where(valid, sp, 0.0)
        sp_hi = sp.astype(BF16)
        sp_lo = (sp - sp_hi.astype(F32)).astype(BF16)
        r_in = (jnp.dot(suffix, sp_hi, preferred_element_type=F32)
                + jnp.dot(suffix, sp_lo, preferred_element_type=F32))
        return z, r_in

    def back(j, h, z, r_in, off, keep):
        a = jnp.where(keep, jnp.exp(z - (r_in + off)), 0.0)
        acc_ref[lanes(h), :] += jnp.dot(vt_ref[j, lanes(h), :], a.astype(BF16), preferred_element_type=F32)

    has_prev = i > 0
    j_prev = jnp.maximum(i - 1, 0)
    own = [front(i, h, True) for h in heads]
    prev = [front(j_prev, h, False) for h in heads]
    offs = []
    for h in heads:
        (z0, r0), (z1, r1) = own[h], prev[h]
        back(i, h, z0, r0, jnp.zeros((1, ATT_TILE), F32), valid)
        off = r0[0:1, :]
        back(j_prev, h, z1, r1, off, has_prev)
        offs.append(off + jnp.where(has_prev, r1[0:1, :], 0.0))
    offs = tuple(offs)

    def least(offs):
        return functools.reduce(jnp.minimum, [jnp.min(off) for off in offs])

    def cond(carry):
        n, _, min_off = carry
        return jnp.logical_and(n < i, min_off <= SB_UNDERFLOW)

    def body(carry):
        n, offs, _ = carry
        j = i - 1 - n
        new = []
        for h in heads:
            z, r_in = front(j, h, False)
            back(j, h, z, r_in, offs[h], True)
            new.append(offs[h] + r_in[0:1, :])
        return n + 1, tuple(new), least(new)

    lax.while_loop(cond, body, (jnp.int32(1), offs, least(offs)))
    o_ref[...] = acc_ref[...].T.astype(BF16)


def _stick_breaking_attention(qk, vt):
    S = qk.shape[0]
    width = SB_HEADS_PER_STEP * LANES
    k_col0 = D_MODEL // width
    return pl.pallas_call(
        _sb_kernel,
        grid=(SB_HEADS // SB_HEADS_PER_STEP, S // ATT_TILE),
        in_specs=[pl.BlockSpec((ATT_TILE, width), lambda g, i: (i, g)),
                  pl.BlockSpec((S, width), lambda g, i: (0, k_col0 + g)),
                  pl.BlockSpec((S // ATT_TILE, width, ATT_TILE), lambda g, i: (0, g, 0))],
        out_specs=pl.BlockSpec((ATT_TILE, width), lambda g, i: (i, g)),
        out_shape=jax.ShapeDtypeStruct((S, D_MODEL), BF16),
        scratch_shapes=[pltpu.VMEM((width, ATT_TILE), F32)],
        compiler_params=_params("parallel", "parallel"),
        name="stick_breaking",
    )(qk, qk, vt)


def _proj_ln_kernel(*refs, n_parts):
    o_refs = refs[:n_parts]
    w_refs = refs[n_parts:2 * n_parts]
    x_ref, g_ref, b_ref, out_ref = refs[2 * n_parts:]
    m = jnp.dot(o_refs[0][...], w_refs[0][...], preferred_element_type=F32)
    for o_ref, w_ref in zip(o_refs[1:], w_refs[1:]):
        m = m + jnp.dot(o_ref[...], w_ref[...], preferred_element_type=F32)
    out_ref[...] = _layer_norm(DEEPNORM_ALPHA * x_ref[...] + m, g_ref[...], b_ref[...])


def _proj_ln(o_parts, w_parts, x, g, b):
    S = x.shape[0]
    n = len(o_parts)
    in_specs = ([pl.BlockSpec((ROW_TILE, o.shape[1]), lambda i: (i, 0)) for o in o_parts]
                + [pl.BlockSpec(w.shape, lambda i: (0, 0)) for w in w_parts]
                + [pl.BlockSpec((ROW_TILE, D_MODEL), lambda i: (i, 0)),
                   pl.BlockSpec((1, D_MODEL), lambda i: (0, 0)),
                   pl.BlockSpec((1, D_MODEL), lambda i: (0, 0))])
    return pl.pallas_call(
        functools.partial(_proj_ln_kernel, n_parts=n),
        grid=(S // ROW_TILE,),
        in_specs=in_specs,
        out_specs=pl.BlockSpec((ROW_TILE, D_MODEL), lambda i: (i, 0)),
        out_shape=jax.ShapeDtypeStruct((S, D_MODEL), F32),
        compiler_params=_params("parallel"),
        name="out_proj_ln",
    )(*o_parts, *w_parts, x, g.reshape(1, D_MODEL), b.reshape(1, D_MODEL))


def _ffn_kernel(x_ref, wup_ref, cw_ref, cb_ref, wd_ref, g_ref, b_ref, out_ref,
                xb_ref, acc_ref, u_ref, h_ref, carry_ref):
    i = pl.program_id(0)
    rows = x_ref.shape[0]
    tf = h_ref.shape[-1]
    n_f = FFN_DIM // tf

    def cols(part, f):
        return slice(part * FFN_DIM + f * tf, part * FFN_DIM + (f + 1) * tf)

    @pl.when(i == 0)
    def _():
        carry_ref[...] = jnp.zeros_like(carry_ref)

    xb_ref[...] = x_ref[...].astype(BF16)
    acc_ref[...] = jnp.zeros_like(acc_ref)
    h_ref[...] = jnp.zeros_like(h_ref)

    def up(f):
        xb = xb_ref[...]
        return [jnp.dot(xb, wup_ref[:, cols(part, f)], preferred_element_type=F32) for part in range(2)]

    def load_u(f, slot, u):
        for part in range(2):
            u_ref[slot, part, 0:SUBLANES, :] = carry_ref[f, part]
            u_ref[slot, part, SUBLANES:SUBLANES + rows, :] = u[part]

    def down(f, slot):
        acc_ref[...] += jnp.dot(h_ref[slot], wd_ref[f * tf:(f + 1) * tf, :], preferred_element_type=F32)

    def gate(f, slot):
        c = []
        for part in range(2):
            cw = cw_ref[:, cols(part, f)]
            acc = cb_ref[:, cols(part, f)]
            for tap in range(CONV_WIDTH):
                start = SUBLANES - (CONV_WIDTH - 1 - tap)
                acc = acc + cw[tap:tap + 1, :] * u_ref[slot, part, start:start + rows, :]
            c.append(acc)
            carry_ref[f, part] = u_ref[slot, part, rows:rows + SUBLANES, :]
        h_ref[slot] = (c[0] * (1.0 / (1.0 + jnp.exp(-c[0]))) * c[1]).astype(BF16)

    def trip(f, slot):
        u_next = up(f + 1)
        down(max(f - 1, 0), 1 - slot)
        gate(f, slot)
        load_u(f + 1, 1 - slot, u_next)

    load_u(0, 0, up(0))
    for f in range(n_f - 1):
        trip(f, f % 2)
    last = n_f - 1
    down(last - 1, 1 - last % 2)
    gate(last, last % 2)
    down(last, last % 2)
    out_ref[...] = _layer_norm(DEEPNORM_ALPHA * x_ref[...] + acc_ref[...], g_ref[...], b_ref[...])


def _conv_ffn_ln(x, w_up, conv_w, conv_b, w_down, g, b):
    S = x.shape[0]
    tm, tf = FFN_ROW_TILE, FFN_COL_TILE
    n_f = FFN_DIM // tf
    whole = lambda a: pl.BlockSpec(a.shape, lambda i: (0,) * a.ndim, pipeline_mode=pl.Buffered(1))
    g2, b2, cb2 = g.reshape(1, D_MODEL), b.reshape(1, D_MODEL), conv_b.reshape(1, 2 * FFN_DIM)
    w_up_b, w_down_b = w_up.astype(BF16), w_down.astype(BF16)
    return pl.pallas_call(
        _ffn_kernel,
        grid=(S // tm,),
        in_specs=[pl.BlockSpec((tm, D_MODEL), lambda i: (i, 0)),
                  whole(w_up_b), whole(conv_w), whole(cb2), whole(w_down_b), whole(g2), whole(b2)],
        out_specs=pl.BlockSpec((tm, D_MODEL), lambda i: (i, 0)),
        out_shape=jax.ShapeDtypeStruct((S, D_MODEL), F32),
        scratch_shapes=[pltpu.VMEM((tm, D_MODEL), BF16),
                        pltpu.VMEM((tm, D_MODEL), F32),
                        pltpu.VMEM((2, 2, tm + 2 * SUBLANES, tf), F32),
                        pltpu.VMEM((2, tm, tf), BF16),
                        pltpu.VMEM((n_f, 2, SUBLANES, tf), F32)],
        compiler_params=_params("arbitrary"),
        name="conv_ffn_ln",
    )(x, w_up_b, conv_w, cb2, w_down_b, g2, b2)


def kernel(x, w_qkv_ab, w_o_ab, w_qkv_c, w_o_c, ln_mix_g, ln_mix_b, w_up, conv_w, conv_b, w_down,
           ln_ffn_g, ln_ffn_b):
    batch, S, _ = x.shape
    assert S % ROW_TILE == 0 and S % FFN_ROW_TILE == 0 and S // MOBA_BLOCK <= HEAD_DIM
    rope_tables = _rope_tables(S)
    moba_width = MOBA_HEADS * HEAD_DIM
    outs = []
    for bi in range(batch):
        h = x[bi]
        for layer in range(DEPTH):
            if layer % 2 == 0:
                qk, vt = _qkv_proj(h, w_qkv_ab[layer // 2], HEAD_DIM ** -0.5 * LOG2_E, rope_tables)
                w_o = w_o_ab[layer // 2].astype(BF16)
                o_parts = [_moba_attention(qk, vt), _dilated_attention(qk, vt)]
                w_parts = [w_o[:moba_width], w_o[moba_width:]]
            else:
                qk, vt = _qkv_proj(h, w_qkv_c[layer // 2], SB_HEAD_DIM ** -0.5)
                o_parts = [_stick_breaking_attention(qk, vt)]
                w_parts = [w_o_c[layer // 2].astype(BF16)]
            h = _proj_ln(o_parts, w_parts, h, ln_mix_g[layer], ln_mix_b[layer])
            h = _conv_ffn_ln(h, w_up[layer], conv_w[layer], conv_b[layer], w_down[layer],
                             ln_ffn_g[layer], ln_ffn_b[layer])
        outs.append(h)
    return jnp.stack(outs)
```

```python
import functools

import numpy as np
import jax
import jax.numpy as jnp
from jax import lax
from jax.experimental import pallas as pl
from jax.experimental.pallas import tpu as pltpu

D_MODEL = 1024
DEPTH = 4
HEAD_DIM = 64
N_HEADS = D_MODEL // HEAD_DIM
MOBA_HEADS = N_HEADS // 4
SB_HEAD_DIM = 128
SB_HEADS = D_MODEL // SB_HEAD_DIM
MOBA_BLOCK = 256
MOBA_TOPK = 3
DILATED_PAIRS = ((128, 1), (512, 4), (2048, 16))
FFN_DIM = ((8 * D_MODEL // 3 + 127) // 128) * 128
CONV_WIDTH = 3
ROPE_THETA = 10000.0
LN_EPS = 1e-5
DEEPNORM_ALPHA = (2 * DEPTH) ** 0.25

LANES = 128
SUBLANES = 8
VMEM_LIMIT_BYTES = 56 * 1024 * 1024

ROW_TILE = 512
FFN_ROW_TILE = 512
FFN_COL_TILE = 256
FFN_DOWN_GROUP = 4
ATT_TILE = 256
ACC_ROWS = HEAD_DIM + 16
MAX_DILATED_WINDOW = max(w for w, _ in DILATED_PAIRS)
DILATED_KEY_TILES = MAX_DILATED_WINDOW // ATT_TILE + 1

NEG = -1e30
LOG2_E = float(np.log2(np.e))
SB_UNDERFLOW = 104.0
F32 = jnp.float32
BF16 = jnp.bfloat16

_CONTRACT_LANES = (((1,), (1,)), ((), ()))


def _params(*semantics, flags=None):
    return pltpu.CompilerParams(dimension_semantics=semantics, vmem_limit_bytes=VMEM_LIMIT_BYTES,
                                flags=flags)


def _layer_norm(y, g, b):
    mu = jnp.mean(y, axis=-1, keepdims=True)
    yc = y - mu
    var = jnp.mean(yc * yc, axis=-1, keepdims=True)
    return yc * lax.rsqrt(var + LN_EPS) * g + b


def _qkv_kernel(*refs, rope, q_scale):
    if rope:
        x_ref, w_ref, wvt_ref, cos_ref, sin_ref, qk_ref, vt_ref = refs
    else:
        x_ref, w_ref, wvt_ref, qk_ref, vt_ref = refs
    xb = x_ref[...].astype(BF16)
    for part in range(2):
        y = jnp.dot(xb, w_ref[:, part * D_MODEL:(part + 1) * D_MODEL], preferred_element_type=F32)
        if rope:
            c = cos_ref[...]
            s = sin_ref[...]
            lane = lax.broadcasted_iota(jnp.int32, c.shape, 1)
            first_half = (lane % HEAD_DIM) < (HEAD_DIM // 2)
            for cb in range(D_MODEL // LANES):
                yc = y[:, cb * LANES:(cb + 1) * LANES]
                partner = jnp.where(first_half,
                                    pltpu.roll(yc, LANES - HEAD_DIM // 2, 1),
                                    pltpu.roll(yc, HEAD_DIM // 2, 1))
                yc = yc * c + partner * s
                if part == 0:
                    yc = yc * q_scale
                qk_ref[:, part * D_MODEL + cb * LANES:part * D_MODEL + (cb + 1) * LANES] = yc.astype(BF16)
        else:
            if part == 0:
                y = y * q_scale
            qk_ref[:, part * D_MODEL:(part + 1) * D_MODEL] = y.astype(BF16)
    vt = lax.dot_general(wvt_ref[...], xb, _CONTRACT_LANES,
                         preferred_element_type=F32).astype(BF16)
    for t in range(ROW_TILE // ATT_TILE):
        vt_ref[t] = vt[:, t * ATT_TILE:(t + 1) * ATT_TILE]


def _qkv_proj(x, w_qkv, q_scale, rope_tables=None):
    w_qk = w_qkv[:, :2 * D_MODEL].astype(BF16)
    w_vt = w_qkv[:, 2 * D_MODEL:].T.astype(BF16)
    S = x.shape[0]
    rope = rope_tables is not None
    tiles = ROW_TILE // ATT_TILE
    in_specs = [pl.BlockSpec((ROW_TILE, D_MODEL), lambda i: (i, 0)),
                pl.BlockSpec((D_MODEL, 2 * D_MODEL), lambda i: (0, 0)),
                pl.BlockSpec((D_MODEL, D_MODEL), lambda i: (0, 0))]
    args = [x, w_qk, w_vt]
    if rope:
        in_specs += [pl.BlockSpec((ROW_TILE, LANES), lambda i: (i, 0))] * 2
        args += list(rope_tables)
    return pl.pallas_call(
        functools.partial(_qkv_kernel, rope=rope, q_scale=q_scale),
        grid=(S // ROW_TILE,),
        in_specs=in_specs,
        out_specs=[pl.BlockSpec((ROW_TILE, 2 * D_MODEL), lambda i: (i, 0)),
                   pl.BlockSpec((tiles, D_MODEL, ATT_TILE), lambda i: (i, 0, 0))],
        out_shape=[jax.ShapeDtypeStruct((S, 2 * D_MODEL), BF16),
                   jax.ShapeDtypeStruct((S // ATT_TILE, D_MODEL, ATT_TILE), BF16)],
        compiler_params=_params("parallel"),
        name="qkv_proj",
    )(*args)


def _rope_tables(S):
    inv = 1.0 / (ROPE_THETA ** (jnp.arange(0, HEAD_DIM, 2, dtype=F32) / HEAD_DIM))
    ang = jnp.arange(S, dtype=F32)[:, None] * inv[None, :]
    cos, sin = jnp.cos(ang), jnp.sin(ang)
    cos_t = jnp.tile(jnp.concatenate([cos, cos], axis=-1), (1, LANES // HEAD_DIM))
    sin_t = jnp.tile(jnp.concatenate([-sin, sin], axis=-1), (1, LANES // HEAD_DIM))
    return cos_t, sin_t


def _softmax_step(s_t, m_old):
    m_new = jnp.maximum(m_old, jnp.max(s_t, axis=0, keepdims=True))
    p_t = jnp.exp2(s_t - m_new)
    alpha = jnp.exp2(m_old - m_new)
    return p_t.astype(BF16), alpha, m_new


def _attention_sweep(first, last, scores, logits, values, acc_ref, s_ref, p_ref):
    acc_ref[...] = jnp.zeros_like(acc_ref)
    p_ref[...] = jnp.zeros_like(p_ref)
    stat0 = jnp.full((1, ATT_TILE), -jnp.inf, F32)
    one = jnp.ones((1, ATT_TILE), F32)
    ones_rows = jnp.ones((ACC_ROWS - HEAD_DIM, ATT_TILE), BF16)
    unrolled = isinstance(first, int) and isinstance(last, int)
    heads = range(s_ref.shape[1])

    def accumulate(t, slot, h, alpha):
        rows = slice(h * ACC_ROWS, (h + 1) * ACC_ROWS)
        lhs = jnp.concatenate([values(t, h), ones_rows], axis=0)
        acc_ref[rows, :] = alpha * acc_ref[rows, :] + jnp.dot(lhs, p_ref[slot, h],
                                                              preferred_element_type=F32)

    def softmax(t, slot, h, stat, is_last):
        p_t, alpha, m_new = _softmax_step(logits(t, h, s_ref[slot, h], is_last), stat)
        p_ref[slot, h] = p_t
        return alpha, m_new

    def trip(t, slot, carry):
        stats, alphas = carry
        prev = max(t - 1, first) if unrolled else jnp.maximum(t - 1, first)
        s_next = [scores(t + 1, h) for h in heads]
        new_stats, new_alphas = [], []
        for h in heads:
            accumulate(prev, 1 - slot, h, alphas[h])
            alpha, stat = softmax(t, slot, h, stats[h], False)
            s_ref[1 - slot, h] = s_next[h]
            new_stats.append(stat)
            new_alphas.append(alpha)
        return tuple(new_stats), tuple(new_alphas)

    def finish(slot, carry):
        stats, alphas = carry
        prev = max(last - 1, first) if unrolled else jnp.maximum(last - 1, first)
        inv = []
        for h in heads:
            accumulate(prev, 1 - slot, h, alphas[h])
            alpha, _ = softmax(last, slot, h, stats[h], True)
            accumulate(last, slot, h, alpha)
            inv.append(1.0 / acc_ref[h * ACC_ROWS + HEAD_DIM:h * ACC_ROWS + HEAD_DIM + 1, :])
        return tuple(inv)

    for h in heads:
        s_ref[0, h] = scores(first, h)
    carry = (tuple(stat0 for _ in heads), tuple(one for _ in heads))
    if unrolled:
        for t in range(first, last):
            carry = trip(t, (t - first) % 2, carry)
        inv = finish((last - first) % 2, carry)
    else:
        n = last - first

        def quad(k, carry):
            t = first + 4 * k
            for d in range(4):
                carry = trip(t + d, d % 2, carry)
            return carry

        def pair(k, carry):
            t = first + (n // 4) * 4 + 2 * k
            return trip(t + 1, 1, trip(t, 0, carry))

        carry = lax.fori_loop(0, n // 4, quad, carry)
        carry = lax.fori_loop(0, (n % 4) // 2, pair, carry)
        inv = lax.cond(n % 2 == 1,
                       lambda c: finish(1, trip(last - 1, 0, c)),
                       lambda c: finish(0, c), carry)
    return jnp.concatenate([acc_ref[h * ACC_ROWS:h * ACC_ROWS + HEAD_DIM, :] * inv[h] for h in heads], axis=0)


def _moba_select_kernel(q_ref, k_ref, qa_ref, ka_ref, km_ref):
    i = pl.program_id(1)
    n_blocks = k_ref.shape[0] // MOBA_BLOCK

    @pl.when(i == 0)
    def _():
        km_ref[...] = jnp.zeros_like(km_ref)

        def body(b, carry):
            kb = k_ref[pl.ds(pl.multiple_of(b * MOBA_BLOCK, MOBA_BLOCK), MOBA_BLOCK), :].astype(F32)
            km_ref[pl.ds(b, 1), :] = jnp.sum(kb, axis=0, keepdims=True) * (1.0 / MOBA_BLOCK)
            return carry

        lax.fori_loop(0, n_blocks, body, 0)

    q = q_ref[...].astype(F32)
    kt = k_ref[pl.ds(pl.multiple_of(i * ATT_TILE, ATT_TILE), ATT_TILE), :].astype(F32)
    lane_i = lax.broadcasted_iota(jnp.int32, q.shape, 1)
    lane = lane_i.astype(F32)
    low = lane_i < HEAD_DIM
    blk = i.astype(F32)
    km = km_ref[...]
    onehot = jnp.where(lane_i - HEAD_DIM == i, 1.0, 0.0)
    q_sw = pltpu.roll(q, HEAD_DIM, 1)
    k_sw = pltpu.roll(kt, HEAD_DIM, 1)
    for h in range(2):
        qh = jnp.where(low if h == 0 else jnp.logical_not(low), q, 0.0)
        gate = lax.dot_general(qh, km, _CONTRACT_LANES, precision=lax.Precision.HIGHEST,
                               preferred_element_type=F32)
        g = jnp.where(lane < blk, gate, -jnp.inf)
        bias = jnp.where(lane == blk, 0.0, NEG)
        for _ in range(MOBA_TOPK):
            m = jnp.max(g, axis=1, keepdims=True)
            first = jnp.min(jnp.where(g == m, lane, float(LANES)), axis=1, keepdims=True)
            first = jnp.where(m > -jnp.inf, first, -1.0)
            pick = lane == first
            bias = jnp.where(pick, 0.0, bias)
            g = jnp.where(pick, -jnp.inf, g)
        bias = pltpu.roll(bias, HEAD_DIM, 1)
        qa_ref[h] = jnp.where(low, q if h == 0 else q_sw, bias).astype(BF16)
        ka_ref[h] = jnp.where(low, kt if h == 0 else k_sw, onehot).astype(BF16)


def _moba_flash_kernel(qa_ref, ka_ref, vt_ref, o_ref, acc_ref, s_ref, p_ref):
    i = pl.program_id(1)
    key = lax.broadcasted_iota(jnp.int32, (ATT_TILE, ATT_TILE), 0)
    qry = lax.broadcasted_iota(jnp.int32, (ATT_TILE, ATT_TILE), 1)

    def scores(j, h):
        off = pl.multiple_of(j * ATT_TILE, ATT_TILE)
        return lax.dot_general(ka_ref[h, pl.ds(off, ATT_TILE), :], qa_ref[h], _CONTRACT_LANES,
                               preferred_element_type=F32)

    def logits(j, h, s_t, is_last):
        return jnp.where(key <= qry, s_t, -jnp.inf) if is_last else s_t

    def values(j, h):
        return vt_ref[j, h * HEAD_DIM:(h + 1) * HEAD_DIM, :]

    o_t = _attention_sweep(0, i, scores, logits, values, acc_ref, s_ref, p_ref)
    o_ref[...] = o_t.T.astype(BF16)


def _moba_attention(qk, vt):
    S = qk.shape[0]
    n_pairs = MOBA_HEADS // 2
    k_col0 = D_MODEL // LANES
    aug = jax.ShapeDtypeStruct((MOBA_HEADS, S, LANES), BF16)
    qa, ka = pl.pallas_call(
        _moba_select_kernel,
        grid=(n_pairs, S // ATT_TILE),
        in_specs=[pl.BlockSpec((ATT_TILE, LANES), lambda p, i: (i, p)),
                  pl.BlockSpec((S, LANES), lambda p, i: (0, k_col0 + p))],
        out_specs=[pl.BlockSpec((2, ATT_TILE, LANES), lambda p, i: (p, i, 0))] * 2,
        out_shape=[aug, aug],
        scratch_shapes=[pltpu.VMEM((LANES, LANES), F32)],
        compiler_params=_params("parallel", "arbitrary"),
        name="moba_select",
    )(qk, qk)
    return pl.pallas_call(
        _moba_flash_kernel,
        grid=(n_pairs, S // ATT_TILE),
        in_specs=[pl.BlockSpec((2, ATT_TILE, LANES), lambda p, i: (p, i, 0)),
                  pl.BlockSpec((2, S, LANES), lambda p, i: (p, 0, 0)),
                  pl.BlockSpec((S // ATT_TILE, LANES, ATT_TILE), lambda p, i: (0, p, 0))],
        out_specs=pl.BlockSpec((ATT_TILE, LANES), lambda p, i: (i, p)),
        out_shape=jax.ShapeDtypeStruct((S, n_pairs * LANES), BF16),
        scratch_shapes=[pltpu.VMEM((2 * ACC_ROWS, ATT_TILE), F32),
                        pltpu.VMEM((2, 2, ATT_TILE, ATT_TILE), F32),
                        pltpu.VMEM((2, 2, ATT_TILE, ATT_TILE), BF16)],
        compiler_params=_params("parallel", "parallel"),
        name="moba_flash",
    )(qa, ka, vt)


def _dilated_bias_table():
    key = np.arange(ATT_TILE)[:, None]
    qry = np.arange(ATT_TILE)[None, :]
    tiles = []
    for t in range(DILATED_KEY_TILES):
        d = qry - key + ATT_TILE * (DILATED_KEY_TILES - 1 - t)
        count = np.zeros_like(d)
        for window, dil in DILATED_PAIRS:
            count += ((d >= 0) & (d <= window) & (d % dil == 0)).astype(d.dtype)
        tiles.append(np.where(count > 0, np.log2(np.maximum(count, 1)), NEG))
    tiles.append(np.full_like(tiles[0], NEG))
    return jnp.asarray(np.stack(tiles), dtype=F32)


DILATED_PAIRS_PER_STEP = 2


def _dilated_kernel(q_ref, k_ref, vt_ref, tab_ref, o_ref, acc_ref, s_ref, p_ref, qh_ref):
    i = pl.program_id(1)
    last = DILATED_KEY_TILES - 1

    def tile(t):
        return jnp.maximum(i - last + t, 0)

    def logits(t, h, s_t, is_last):
        return s_t + tab_ref[jnp.where(i - last + t < 0, DILATED_KEY_TILES, t)]

    for pair in range(DILATED_PAIRS_PER_STEP):
        lanes = slice(pair * LANES, (pair + 1) * LANES)
        q = q_ref[:, lanes]
        low = lax.broadcasted_iota(jnp.int32, q.shape, 1) < HEAD_DIM
        zero = jnp.zeros_like(q)
        qh_ref[pair, 0] = jnp.where(low, q, zero)
        qh_ref[pair, 1] = jnp.where(low, zero, q)

        def scores(t, h, pair=pair, lanes=lanes):
            off = pl.multiple_of(tile(t) * ATT_TILE, ATT_TILE)
            return lax.dot_general(k_ref[pl.ds(off, ATT_TILE), lanes], qh_ref[pair, h], _CONTRACT_LANES,
                                   preferred_element_type=F32)

        def values(t, h, pair=pair):
            row0 = pair * LANES + h * HEAD_DIM
            return vt_ref[tile(t), row0:row0 + HEAD_DIM, :]

        o_t = _attention_sweep(0, last, scores, logits, values, acc_ref.at[pair], s_ref.at[pair], p_ref.at[pair])
        o_ref[:, lanes] = o_t.T.astype(BF16)


def _dilated_attention(qk, vt):
    S = qk.shape[0]
    width = DILATED_PAIRS_PER_STEP * LANES
    n_groups = (N_HEADS - MOBA_HEADS) * HEAD_DIM // width
    q_col0 = MOBA_HEADS * HEAD_DIM // width
    k_col0 = D_MODEL // width + q_col0
    per = DILATED_PAIRS_PER_STEP
    return pl.pallas_call(
        _dilated_kernel,
        grid=(n_groups, S // ATT_TILE),
        in_specs=[pl.BlockSpec((ATT_TILE, width), lambda g, i: (i, q_col0 + g)),
                  pl.BlockSpec((S, width), lambda g, i: (0, k_col0 + g)),
                  pl.BlockSpec((S // ATT_TILE, width, ATT_TILE), lambda g, i: (0, q_col0 + g, 0)),
                  pl.BlockSpec((DILATED_KEY_TILES + 1, ATT_TILE, ATT_TILE), lambda g, i: (0, 0, 0))],
        out_specs=pl.BlockSpec((ATT_TILE, width), lambda g, i: (i, g)),
        out_shape=jax.ShapeDtypeStruct((S, n_groups * width), BF16),
        scratch_shapes=[pltpu.VMEM((per, 2 * ACC_ROWS, ATT_TILE), F32),
                        pltpu.VMEM((per, 2, 2, ATT_TILE, ATT_TILE), F32),
                        pltpu.VMEM((per, 2, 2, ATT_TILE, ATT_TILE), BF16),
                        pltpu.VMEM((per, 2, ATT_TILE, LANES), BF16)],
        compiler_params=_params("parallel", "parallel"),
        name="dilated_attention",
    )(qk, qk, vt, _dilated_bias_table())


SB_HEADS_PER_STEP = 2


def _sb_kernel(q_ref, k_ref, vt_ref, o_ref, acc_ref):
    i = pl.program_id(1)
    heads = range(SB_HEADS_PER_STEP)
    key = lax.broadcasted_iota(jnp.int32, (ATT_TILE, ATT_TILE), 0)
    qry = lax.broadcasted_iota(jnp.int32, (ATT_TILE, ATT_TILE), 1)
    suffix = jnp.where(qry >= key, 1.0, 0.0).astype(BF16)
    acc_ref[...] = jnp.zeros_like(acc_ref)
    valid = key < qry

    def lanes(h):
        return slice(h * LANES, (h + 1) * LANES)

    def front(j, h, diagonal):
        k = k_ref[pl.ds(pl.multiple_of(j * ATT_TILE, ATT_TILE), ATT_TILE), lanes(h)]
        z = lax.dot_general(k, q_ref[:, lanes(h)], _CONTRACT_LANES, preferred_element_type=F32)
        sp = jnp.maximum(z, 0.0) + jnp.log(1.0 + jnp.exp(-jnp.abs(z)))
        if diagonal:
            sp = jnp.where(valid, sp, 0.0)
        sp_hi = sp.astype(BF16)
        sp_lo = (sp - sp_hi.astype(F32)).astype(BF16)
        r_in = (jnp.dot(suffix, sp_hi, preferred_element_type=F32)
                + jnp.dot(suffix, sp_lo, preferred_element_type=F32))
        return z, r_in

    def back(j, h, z, r_in, off, keep):
        a = jnp.where(keep, jnp.exp(z - (r_in + off)), 0.0)
        acc_ref[lanes(h), :] += jnp.dot(vt_ref[j, lanes(h), :], a.astype(BF16), preferred_element_type=F32)

    has_prev = i > 0
    j_prev = jnp.maximum(i - 1, 0)
    own = [front(i, h, True) for h in heads]
    prev = [front(j_prev, h, False) for h in heads]
    offs = []
    for h in heads:
        (z0, r0), (z1, r1) = own[h], prev[h]
        back(i, h, z0, r0, jnp.zeros((1, ATT_TILE), F32), valid)
        off = r0[0:1, :]
        back(j_prev, h, z1, r1, off, has_prev)
        offs.append(off + jnp.where(has_prev, r1[0:1, :], 0.0))
    offs = tuple(offs)

    def least(offs):
        return functools.reduce(jnp.minimum, [jnp.min(off) for off in offs])

    def cond(carry):
        n, _, min_off = carry
        return jnp.logical_and(n < i, min_off <= SB_UNDERFLOW)

    def body(carry):
        n, offs, _ = carry
        j = i - 1 - n
        new = []
        for h in heads:
            z, r_in = front(j, h, False)
            back(j, h, z, r_in, offs[h], True)
            new.append(offs[h] + r_in[0:1, :])
        return n + 1, tuple(new), least(new)

    lax.while_loop(cond, body, (jnp.int32(1), offs, least(offs)))
    o_ref[...] = acc_ref[...].T.astype(BF16)


def _stick_breaking_attention(qk, vt):
    S = qk.shape[0]
    width = SB_HEADS_PER_STEP * LANES
    k_col0 = D_MODEL // width
    return pl.pallas_call(
        _sb_kernel,
        grid=(SB_HEADS // SB_HEADS_PER_STEP, S // ATT_TILE),
        in_specs=[pl.BlockSpec((ATT_TILE, width), lambda g, i: (i, g)),
                  pl.BlockSpec((S, width), lambda g, i: (0, k_col0 + g)),
                  pl.BlockSpec((S // ATT_TILE, width, ATT_TILE), lambda g, i: (0, g, 0))],
        out_specs=pl.BlockSpec((ATT_TILE, width), lambda g, i: (i, g)),
        out_shape=jax.ShapeDtypeStruct((S, D_MODEL), BF16),
        scratch_shapes=[pltpu.VMEM((width, ATT_TILE), F32)],
        compiler_params=_params("parallel", "parallel"),
        name="stick_breaking",
    )(qk, qk, vt)


def _proj_ln_kernel(*refs, n_parts):
    o_refs = refs[:n_parts]
    w_refs = refs[n_parts:2 * n_parts]
    x_ref, g_ref, b_ref, out_ref = refs[2 * n_parts:]
    m = jnp.dot(o_refs[0][...], w_refs[0][...], preferred_element_type=F32)
    for o_ref, w_ref in zip(o_refs[1:], w_refs[1:]):
        m = m + jnp.dot(o_ref[...], w_ref[...], preferred_element_type=F32)
    out_ref[...] = _layer_norm(DEEPNORM_ALPHA * x_ref[...] + m, g_ref[...], b_ref[...])


def _proj_ln(o_parts, w_parts, x, g, b):
    S = x.shape[0]
    n = len(o_parts)
    in_specs = ([pl.BlockSpec((ROW_TILE, o.shape[1]), lambda i: (i, 0)) for o in o_parts]
                + [pl.BlockSpec(w.shape, lambda i: (0, 0)) for w in w_parts]
                + [pl.BlockSpec((ROW_TILE, D_MODEL), lambda i: (i, 0)),
                   pl.BlockSpec((1, D_MODEL), lambda i: (0, 0)),
                   pl.BlockSpec((1, D_MODEL), lambda i: (0, 0))])
    return pl.pallas_call(
        functools.partial(_proj_ln_kernel, n_parts=n),
        grid=(S // ROW_TILE,),
        in_specs=in_specs,
        out_specs=pl.BlockSpec((ROW_TILE, D_MODEL), lambda i: (i, 0)),
        out_shape=jax.ShapeDtypeStruct((S, D_MODEL), F32),
        compiler_params=_params("parallel"),
        name="out_proj_ln",
    )(*o_parts, *w_parts, x, g.reshape(1, D_MODEL), b.reshape(1, D_MODEL))


def _ffn_kernel(x_ref, wup_ref, cw_ref, cb_ref, wd_ref, g_ref, b_ref, out_ref,
                xb_ref, acc_ref, u_ref, h_ref, carry_ref):
    i = pl.program_id(0)
    rows = x_ref.shape[0]
    tf = u_ref.shape[-1]
    n_f = FFN_DIM // tf

    def cols(part, f):
        return slice(part * FFN_DIM + f * tf, part * FFN_DIM + (f + 1) * tf)

    @pl.when(i == 0)
    def _():
        carry_ref[...] = jnp.zeros_like(carry_ref)

    xb_ref[...] = x_ref[...].astype(BF16)

    def up(f):
        xb = xb_ref[...]
        return [jnp.dot(xb, wup_ref[:, cols(part, f)], preferred_element_type=F32) for part in range(2)]

    def load_u(f, slot, u):
        for part in range(2):
            u_ref[slot, part, 0:SUBLANES, :] = carry_ref[f, part]
            u_ref[slot, part, SUBLANES:SUBLANES + rows, :] = u[part]

    def down(f0, f1):
        part = jnp.dot(h_ref[:, f0 * tf:f1 * tf], wd_ref[f0 * tf:f1 * tf, :], preferred_element_type=F32)
        acc_ref[...] = part if f0 == 0 else acc_ref[...] + part

    def gate(f, slot):
        c = []
        for part in range(2):
            cw = cw_ref[:, cols(part, f)]
            acc = cb_ref[:, cols(part, f)]
            for tap in range(CONV_WIDTH):
                start = SUBLANES - (CONV_WIDTH - 1 - tap)
                acc = acc + cw[tap:tap + 1, :] * u_ref[slot, part, start:start + rows, :]
            c.append(acc)
            carry_ref[f, part] = u_ref[slot, part, rows:rows + SUBLANES, :]
        h_ref[:, f * tf:(f + 1) * tf] = (c[0] * (1.0 / (1.0 + jnp.exp(-c[0]))) * c[1]).astype(BF16)

    load_u(0, 0, up(0))
    done = 0
    for f in range(n_f):
        if f + 1 < n_f:
            u_next = up(f + 1)
        gate(f, f % 2)
        if f + 1 < n_f:
            load_u(f + 1, 1 - f % 2, u_next)
        if (f + 1) % FFN_DOWN_GROUP == 0 or f + 1 == n_f:
            down(done, f + 1)
            done = f + 1
    out_ref[...] = _layer_norm(DEEPNORM_ALPHA * x_ref[...] + acc_ref[...], g_ref[...], b_ref[...])


def _conv_ffn_ln(x, w_up, conv_w, conv_b, w_down, g, b):
    S = x.shape[0]
    tm, tf = FFN_ROW_TILE, FFN_COL_TILE
    n_f = FFN_DIM // tf
    whole = lambda a: pl.BlockSpec(a.shape, lambda i: (0,) * a.ndim, pipeline_mode=pl.Buffered(1))
    g2, b2, cb2 = g.reshape(1, D_MODEL), b.reshape(1, D_MODEL), conv_b.reshape(1, 2 * FFN_DIM)
    w_up_b, w_down_b = w_up.astype(BF16), w_down.astype(BF16)
    return pl.pallas_call(
        _ffn_kernel,
        grid=(S // tm,),
        in_specs=[pl.BlockSpec((tm, D_MODEL), lambda i: (i, 0)),
                  whole(w_up_b), whole(conv_w), whole(cb2), whole(w_down_b), whole(g2), whole(b2)],
        out_specs=pl.BlockSpec((tm, D_MODEL), lambda i: (i, 0)),
        out_shape=jax.ShapeDtypeStruct((S, D_MODEL), F32),
        scratch_shapes=[pltpu.VMEM((tm, D_MODEL), BF16),
                        pltpu.VMEM((tm, D_MODEL), F32),
                        pltpu.VMEM((2, 2, tm + 2 * SUBLANES, tf), F32),
                        pltpu.VMEM((tm, FFN_DIM), BF16),
                        pltpu.VMEM((n_f, 2, SUBLANES, tf), F32)],
        compiler_params=_params("arbitrary"),
        name="conv_ffn_ln",
    )(x, w_up_b, conv_w, cb2, w_down_b, g2, b2)


def kernel(x, w_qkv_ab, w_o_ab, w_qkv_c, w_o_c, ln_mix_g, ln_mix_b, w_up, conv_w, conv_b, w_down,
           ln_ffn_g, ln_ffn_b):
    batch, S, _ = x.shape
    assert S % ROW_TILE == 0 and S % FFN_ROW_TILE == 0 and S // MOBA_BLOCK <= HEAD_DIM
    rope_tables = _rope_tables(S)
    moba_width = MOBA_HEADS * HEAD_DIM
    outs = []
    for bi in range(batch):
        h = x[bi]
        for layer in range(DEPTH):
            if layer % 2 == 0:
                qk, vt = _qkv_proj(h, w_qkv_ab[layer // 2], HEAD_DIM ** -0.5 * LOG2_E, rope_tables)
                w_o = w_o_ab[layer // 2].astype(BF16)
                o_parts = [_moba_attention(qk, vt), _dilated_attention(qk, vt)]
                w_parts = [w_o[:moba_width], w_o[moba_width:]]
            else:
                qk, vt = _qkv_proj(h, w_qkv_c[layer // 2], SB_HEAD_DIM ** -0.5)
                o_parts = [_stick_breaking_attention(qk, vt)]
                w_parts = [w_o_c[layer // 2].astype(BF16)]
            h = _proj_ln(o_parts, w_parts, h, ln_mix_g[layer], ln_mix_b[layer])
            h = _conv_ffn_ln(h, w_up[layer], conv_w[layer], conv_b[layer], w_down[layer],
                             ln_ffn_g[layer], ln_ffn_b[layer])
        outs.append(h)
    return jnp.stack(outs)
```

```python
import functools

import numpy as np
import jax
import jax.numpy as jnp
from jax import lax
from jax.experimental import pallas as pl
from jax.experimental.pallas import tpu as pltpu

D_MODEL = 1024
DEPTH = 4
HEAD_DIM = 64
N_HEADS = D_MODEL // HEAD_DIM
MOBA_HEADS = N_HEADS // 4
SB_HEAD_DIM = 128
SB_HEADS = D_MODEL // SB_HEAD_DIM
MOBA_BLOCK = 256
MOBA_TOPK = 3
DILATED_PAIRS = ((128, 1), (512, 4), (2048, 16))
FFN_DIM = ((8 * D_MODEL // 3 + 127) // 128) * 128
CONV_WIDTH = 3
ROPE_THETA = 10000.0
LN_EPS = 1e-5
DEEPNORM_ALPHA = (2 * DEPTH) ** 0.25

LANES = 128
SUBLANES = 8
VMEM_LIMIT_BYTES = 56 * 1024 * 1024

ROW_TILE = 512
FFN_ROW_TILE = 512
FFN_COL_TILE = 256
FFN_DOWN_GROUP = 4
ATT_TILE = 256
ACC_ROWS = HEAD_DIM + 16
MAX_DILATED_WINDOW = max(w for w, _ in DILATED_PAIRS)
DILATED_KEY_TILES = MAX_DILATED_WINDOW // ATT_TILE + 1

NEG = -1e30
LOG2_E = float(np.log2(np.e))
SB_UNDERFLOW = 104.0
F32 = jnp.float32
BF16 = jnp.bfloat16

_CONTRACT_LANES = (((1,), (1,)), ((), ()))


def _params(*semantics, flags=None):
    return pltpu.CompilerParams(dimension_semantics=semantics, vmem_limit_bytes=VMEM_LIMIT_BYTES,
                                flags=flags)


def _layer_norm(y, g, b):
    mu = jnp.mean(y, axis=-1, keepdims=True)
    yc = y - mu
    var = jnp.mean(yc * yc, axis=-1, keepdims=True)
    return yc * lax.rsqrt(var + LN_EPS) * g + b


def _qkv_kernel(*refs, rope, q_scale):
    if rope:
        x_ref, w_ref, wvt_ref, cos_ref, sin_ref, qk_ref, vt_ref = refs
    else:
        x_ref, w_ref, wvt_ref, qk_ref, vt_ref = refs
    xb = x_ref[...].astype(BF16)
    for part in range(2):
        y = jnp.dot(xb, w_ref[:, part * D_MODEL:(part + 1) * D_MODEL], preferred_element_type=F32)
        if rope:
            c = cos_ref[...]
            s = sin_ref[...]
            lane = lax.broadcasted_iota(jnp.int32, c.shape, 1)
            first_half = (lane % HEAD_DIM) < (HEAD_DIM // 2)
            for cb in range(D_MODEL // LANES):
                yc = y[:, cb * LANES:(cb + 1) * LANES]
                partner = jnp.where(first_half,
                                    pltpu.roll(yc, LANES - HEAD_DIM // 2, 1),
                                    pltpu.roll(yc, HEAD_DIM // 2, 1))
                yc = yc * c + partner * s
                if part == 0:
                    yc = yc * q_scale
                qk_ref[:, part * D_MODEL + cb * LANES:part * D_MODEL + (cb + 1) * LANES] = yc.astype(BF16)
        else:
            if part == 0:
                y = y * q_scale
            qk_ref[:, part * D_MODEL:(part + 1) * D_MODEL] = y.astype(BF16)
    vt = lax.dot_general(wvt_ref[...], xb, _CONTRACT_LANES,
                         preferred_element_type=F32).astype(BF16)
    for t in range(ROW_TILE // ATT_TILE):
        vt_ref[t] = vt[:, t * ATT_TILE:(t + 1) * ATT_TILE]


def _qkv_proj(x, w_qkv, q_scale, rope_tables=None):
    w_qk = w_qkv[:, :2 * D_MODEL].astype(BF16)
    w_vt = w_qkv[:, 2 * D_MODEL:].T.astype(BF16)
    S = x.shape[0]
    rope = rope_tables is not None
    tiles = ROW_TILE // ATT_TILE
    in_specs = [pl.BlockSpec((ROW_TILE, D_MODEL), lambda i: (i, 0)),
                pl.BlockSpec((D_MODEL, 2 * D_MODEL), lambda i: (0, 0)),
                pl.BlockSpec((D_MODEL, D_MODEL), lambda i: (0, 0))]
    args = [x, w_qk, w_vt]
    if rope:
        in_specs += [pl.BlockSpec((ROW_TILE, LANES), lambda i: (i, 0))] * 2
        args += list(rope_tables)
    return pl.pallas_call(
        functools.partial(_qkv_kernel, rope=rope, q_scale=q_scale),
        grid=(S // ROW_TILE,),
        in_specs=in_specs,
        out_specs=[pl.BlockSpec((ROW_TILE, 2 * D_MODEL), lambda i: (i, 0)),
                   pl.BlockSpec((tiles, D_MODEL, ATT_TILE), lambda i: (i, 0, 0))],
        out_shape=[jax.ShapeDtypeStruct((S, 2 * D_MODEL), BF16),
                   jax.ShapeDtypeStruct((S // ATT_TILE, D_MODEL, ATT_TILE), BF16)],
        compiler_params=_params("parallel"),
        name="qkv_proj",
    )(*args)


def _rope_tables(S):
    inv = 1.0 / (ROPE_THETA ** (jnp.arange(0, HEAD_DIM, 2, dtype=F32) / HEAD_DIM))
    ang = jnp.arange(S, dtype=F32)[:, None] * inv[None, :]
    cos, sin = jnp.cos(ang), jnp.sin(ang)
    cos_t = jnp.tile(jnp.concatenate([cos, cos], axis=-1), (1, LANES // HEAD_DIM))
    sin_t = jnp.tile(jnp.concatenate([-sin, sin], axis=-1), (1, LANES // HEAD_DIM))
    return cos_t, sin_t


def _softmax_step(s_t, m_old):
    m_new = jnp.maximum(m_old, jnp.max(s_t, axis=0, keepdims=True))
    p_t = jnp.exp2(s_t - m_new)
    alpha = jnp.exp2(m_old - m_new)
    return p_t.astype(BF16), alpha, m_new


def _attention_sweep(first, last, scores, logits, values, acc_ref, s_ref, p_ref):
    acc_ref[...] = jnp.zeros_like(acc_ref)
    p_ref[...] = jnp.zeros_like(p_ref)
    stat0 = jnp.full((1, ATT_TILE), -jnp.inf, F32)
    one = jnp.ones((1, ATT_TILE), F32)
    ones_rows = jnp.ones((ACC_ROWS - HEAD_DIM, ATT_TILE), BF16)
    unrolled = isinstance(first, int) and isinstance(last, int)
    heads = range(s_ref.shape[1])

    def accumulate(t, slot, h, alpha):
        rows = slice(h * ACC_ROWS, (h + 1) * ACC_ROWS)
        lhs = jnp.concatenate([values(t, h), ones_rows], axis=0)
        acc_ref[rows, :] = alpha * acc_ref[rows, :] + jnp.dot(lhs, p_ref[slot, h],
                                                              preferred_element_type=F32)

    def softmax(t, slot, h, stat, is_last):
        p_t, alpha, m_new = _softmax_step(logits(t, h, s_ref[slot, h], is_last), stat)
        p_ref[slot, h] = p_t
        return alpha, m_new

    def trip(t, slot, carry):
        stats, alphas = carry
        prev = max(t - 1, first) if unrolled else jnp.maximum(t - 1, first)
        s_next = [scores(t + 1, h) for h in heads]
        new_stats, new_alphas = [], []
        for h in heads:
            accumulate(prev, 1 - slot, h, alphas[h])
            alpha, stat = softmax(t, slot, h, stats[h], False)
            s_ref[1 - slot, h] = s_next[h]
            new_stats.append(stat)
            new_alphas.append(alpha)
        return tuple(new_stats), tuple(new_alphas)

    def finish(slot, carry):
        stats, alphas = carry
        prev = max(last - 1, first) if unrolled else jnp.maximum(last - 1, first)
        inv = []
        for h in heads:
            accumulate(prev, 1 - slot, h, alphas[h])
            alpha, _ = softmax(last, slot, h, stats[h], True)
            accumulate(last, slot, h, alpha)
            inv.append(1.0 / acc_ref[h * ACC_ROWS + HEAD_DIM:h * ACC_ROWS + HEAD_DIM + 1, :])
        return tuple(inv)

    for h in heads:
        s_ref[0, h] = scores(first, h)
    carry = (tuple(stat0 for _ in heads), tuple(one for _ in heads))
    if unrolled:
        for t in range(first, last):
            carry = trip(t, (t - first) % 2, carry)
        inv = finish((last - first) % 2, carry)
    else:
        n = last - first

        def quad(k, carry):
            t = first + 4 * k
            for d in range(4):
                carry = trip(t + d, d % 2, carry)
            return carry

        def pair(k, carry):
            t = first + (n // 4) * 4 + 2 * k
            return trip(t + 1, 1, trip(t, 0, carry))

        carry = lax.fori_loop(0, n // 4, quad, carry)
        carry = lax.fori_loop(0, (n % 4) // 2, pair, carry)
        inv = lax.cond(n % 2 == 1,
                       lambda c: finish(1, trip(last - 1, 0, c)),
                       lambda c: finish(0, c), carry)
    return jnp.concatenate([acc_ref[h * ACC_ROWS:h * ACC_ROWS + HEAD_DIM, :] * inv[h] for h in heads], axis=0)


def _moba_select_kernel(q_ref, k_ref, qa_ref, ka_ref, km_ref):
    i = pl.program_id(1)
    n_blocks = k_ref.shape[0] // MOBA_BLOCK

    @pl.when(i == 0)
    def _():
        km_ref[...] = jnp.zeros_like(km_ref)

        def body(b, carry):
            kb = k_ref[pl.ds(pl.multiple_of(b * MOBA_BLOCK, MOBA_BLOCK), MOBA_BLOCK), :].astype(F32)
            km_ref[pl.ds(b, 1), :] = jnp.sum(kb, axis=0, keepdims=True) * (1.0 / MOBA_BLOCK)
            return carry

        lax.fori_loop(0, n_blocks, body, 0)

    q = q_ref[...].astype(F32)
    kt = k_ref[pl.ds(pl.multiple_of(i * ATT_TILE, ATT_TILE), ATT_TILE), :].astype(F32)
    lane_i = lax.broadcasted_iota(jnp.int32, q.shape, 1)
    lane = lane_i.astype(F32)
    low = lane_i < HEAD_DIM
    blk = i.astype(F32)
    km = km_ref[...]
    onehot = jnp.where(lane_i - HEAD_DIM == i, 1.0, 0.0)
    q_sw = pltpu.roll(q, HEAD_DIM, 1)
    k_sw = pltpu.roll(kt, HEAD_DIM, 1)
    for h in range(2):
        qh = jnp.where(low if h == 0 else jnp.logical_not(low), q, 0.0)
        gate = lax.dot_general(qh, km, _CONTRACT_LANES, precision=lax.Precision.HIGHEST,
                               preferred_element_type=F32)
        g = jnp.where(lane < blk, gate, -jnp.inf)
        bias = jnp.where(lane == blk, 0.0, NEG)
        for _ in range(MOBA_TOPK):
            m = jnp.max(g, axis=1, keepdims=True)
            first = jnp.min(jnp.where(g == m, lane, float(LANES)), axis=1, keepdims=True)
            first = jnp.where(m > -jnp.inf, first, -1.0)
            pick = lane == first
            bias = jnp.where(pick, 0.0, bias)
            g = jnp.where(pick, -jnp.inf, g)
        bias = pltpu.roll(bias, HEAD_DIM, 1)
        qa_ref[h] = jnp.where(low, q if h == 0 else q_sw, bias).astype(BF16)
        ka_ref[h] = jnp.where(low, kt if h == 0 else k_sw, onehot).astype(BF16)


def _moba_flash_kernel(qa_ref, ka_ref, vt_ref, o_ref, acc_ref, s_ref, p_ref):
    i = pl.program_id(1)
    key = lax.broadcasted_iota(jnp.int32, (ATT_TILE, ATT_TILE), 0)
    qry = lax.broadcasted_iota(jnp.int32, (ATT_TILE, ATT_TILE), 1)

    def scores(j, h):
        off = pl.multiple_of(j * ATT_TILE, ATT_TILE)
        return lax.dot_general(ka_ref[h, pl.ds(off, ATT_TILE), :], qa_ref[h], _CONTRACT_LANES,
                               preferred_element_type=F32)

    def logits(j, h, s_t, is_last):
        return jnp.where(key <= qry, s_t, -jnp.inf) if is_last else s_t

    def values(j, h):
        return vt_ref[j, h * HEAD_DIM:(h + 1) * HEAD_DIM, :]

    o_t = _attention_sweep(0, i, scores, logits, values, acc_ref, s_ref, p_ref)
    o_ref[...] = o_t.T.astype(BF16)


def _moba_attention(qk, vt):
    S = qk.shape[0]
    n_pairs = MOBA_HEADS // 2
    k_col0 = D_MODEL // LANES
    aug = jax.ShapeDtypeStruct((MOBA_HEADS, S, LANES), BF16)
    qa, ka = pl.pallas_call(
        _moba_select_kernel,
        grid=(n_pairs, S // ATT_TILE),
        in_specs=[pl.BlockSpec((ATT_TILE, LANES), lambda p, i: (i, p)),
                  pl.BlockSpec((S, LANES), lambda p, i: (0, k_col0 + p))],
        out_specs=[pl.BlockSpec((2, ATT_TILE, LANES), lambda p, i: (p, i, 0))] * 2,
        out_shape=[aug, aug],
        scratch_shapes=[pltpu.VMEM((LANES, LANES), F32)],
        compiler_params=_params("parallel", "arbitrary"),
        name="moba_select",
    )(qk, qk)
    return pl.pallas_call(
        _moba_flash_kernel,
        grid=(n_pairs, S // ATT_TILE),
        in_specs=[pl.BlockSpec((2, ATT_TILE, LANES), lambda p, i: (p, i, 0)),
                  pl.BlockSpec((2, S, LANES), lambda p, i: (p, 0, 0)),
                  pl.BlockSpec((S // ATT_TILE, LANES, ATT_TILE), lambda p, i: (0, p, 0))],
        out_specs=pl.BlockSpec((ATT_TILE, LANES), lambda p, i: (i, p)),
        out_shape=jax.ShapeDtypeStruct((S, n_pairs * LANES), BF16),
        scratch_shapes=[pltpu.VMEM((2 * ACC_ROWS, ATT_TILE), F32),
                        pltpu.VMEM((2, 2, ATT_TILE, ATT_TILE), F32),
                        pltpu.VMEM((2, 2, ATT_TILE, ATT_TILE), BF16)],
        compiler_params=_params("parallel", "parallel"),
        name="moba_flash",
    )(qa, ka, vt)


def _dilated_bias_table():
    key = np.arange(ATT_TILE)[:, None]
    qry = np.arange(ATT_TILE)[None, :]
    tiles = []
    for t in range(DILATED_KEY_TILES):
        d = qry - key + ATT_TILE * (DILATED_KEY_TILES - 1 - t)
        count = np.zeros_like(d)
        for window, dil in DILATED_PAIRS:
            count += ((d >= 0) & (d <= window) & (d % dil == 0)).astype(d.dtype)
        tiles.append(np.where(count > 0, np.log2(np.maximum(count, 1)), NEG))
    tiles.append(np.full_like(tiles[0], NEG))
    return jnp.asarray(np.stack(tiles), dtype=F32)


DILATED_PAIRS_PER_STEP = 2


def _dilated_kernel(q_ref, k_ref, vt_ref, tab_ref, o_ref, acc_ref, s_ref, p_ref, qh_ref):
    i = pl.program_id(1)
    last = DILATED_KEY_TILES - 1

    def tile(t):
        return jnp.maximum(i - last + t, 0)

    def logits(t, h, s_t, is_last):
        return s_t + tab_ref[jnp.where(i - last + t < 0, DILATED_KEY_TILES, t)]

    for pair in range(DILATED_PAIRS_PER_STEP):
        lanes = slice(pair * LANES, (pair + 1) * LANES)
        q = q_ref[:, lanes]
        low = lax.broadcasted_iota(jnp.int32, q.shape, 1) < HEAD_DIM
        zero = jnp.zeros_like(q)
        qh_ref[pair, 0] = jnp.where(low, q, zero)
        qh_ref[pair, 1] = jnp.where(low, zero, q)

        def scores(t, h, pair=pair, lanes=lanes):
            off = pl.multiple_of(tile(t) * ATT_TILE, ATT_TILE)
            return lax.dot_general(k_ref[pl.ds(off, ATT_TILE), lanes], qh_ref[pair, h], _CONTRACT_LANES,
                                   preferred_element_type=F32)

        def values(t, h, pair=pair):
            row0 = pair * LANES + h * HEAD_DIM
            return vt_ref[tile(t), row0:row0 + HEAD_DIM, :]

        o_t = _attention_sweep(0, last, scores, logits, values, acc_ref.at[pair], s_ref.at[pair], p_ref.at[pair])
        o_ref[:, lanes] = o_t.T.astype(BF16)


def _dilated_attention(qk, vt):
    S = qk.shape[0]
    width = DILATED_PAIRS_PER_STEP * LANES
    n_groups = (N_HEADS - MOBA_HEADS) * HEAD_DIM // width
    q_col0 = MOBA_HEADS * HEAD_DIM // width
    k_col0 = D_MODEL // width + q_col0
    per = DILATED_PAIRS_PER_STEP
    return pl.pallas_call(
        _dilated_kernel,
        grid=(n_groups, S // ATT_TILE),
        in_specs=[pl.BlockSpec((ATT_TILE, width), lambda g, i: (i, q_col0 + g)),
                  pl.BlockSpec((S, width), lambda g, i: (0, k_col0 + g)),
                  pl.BlockSpec((S // ATT_TILE, width, ATT_TILE), lambda g, i: (0, q_col0 + g, 0)),
                  pl.BlockSpec((DILATED_KEY_TILES + 1, ATT_TILE, ATT_TILE), lambda g, i: (0, 0, 0))],
        out_specs=pl.BlockSpec((ATT_TILE, width), lambda g, i: (i, g)),
        out_shape=jax.ShapeDtypeStruct((S, n_groups * width), BF16),
        scratch_shapes=[pltpu.VMEM((per, 2 * ACC_ROWS, ATT_TILE), F32),
                        pltpu.VMEM((per, 2, 2, ATT_TILE, ATT_TILE), F32),
                        pltpu.VMEM((per, 2, 2, ATT_TILE, ATT_TILE), BF16),
                        pltpu.VMEM((per, 2, ATT_TILE, LANES), BF16)],
        compiler_params=_params("parallel", "parallel"),
        name="dilated_attention",
    )(qk, qk, vt, _dilated_bias_table())


SB_HEADS_PER_STEP = 4


def _sb_kernel(q_ref, k_ref, vt_ref, o_ref, acc_ref):
    i = pl.program_id(1)
    heads = range(SB_HEADS_PER_STEP)
    key = lax.broadcasted_iota(jnp.int32, (ATT_TILE, ATT_TILE), 0)
    qry = lax.broadcasted_iota(jnp.int32, (ATT_TILE, ATT_TILE), 1)
    suffix = jnp.where(qry >= key, 1.0, 0.0).astype(BF16)
    acc_ref[...] = jnp.zeros_like(acc_ref)
    valid = key < qry

    def lanes(h):
        return slice(h * LANES, (h + 1) * LANES)

    def front(j, h, diagonal):
        k = k_ref[pl.ds(pl.multiple_of(j * ATT_TILE, ATT_TILE), ATT_TILE), lanes(h)]
        z = lax.dot_general(k, q_ref[:, lanes(h)], _CONTRACT_LANES, preferred_element_type=F32)
        sp = jnp.maximum(z, 0.0) + jnp.log(1.0 + jnp.exp(-jnp.abs(z)))
        if diagonal:
            sp = jnp.where(valid, sp, 0.0)
        sp_hi = sp.astype(BF16)
        sp_lo = (sp - sp_hi.astype(F32)).astype(BF16)
        r_in = (jnp.dot(suffix, sp_hi, preferred_element_type=F32)
                + jnp.dot(suffix, sp_lo, preferred_element_type=F32))
        return z, r_in

    def back(j, h, z, r_in, off, keep):
        a = jnp.where(keep, jnp.exp(z - (r_in + off)), 0.0)
        acc_ref[lanes(h), :] += jnp.dot(vt_ref[j, lanes(h), :], a.astype(BF16), preferred_element_type=F32)

    has_prev = i > 0
    j_prev = jnp.maximum(i - 1, 0)
    own = [front(i, h, True) for h in heads]
    prev = [front(j_prev, h, False) for h in heads]
    offs = []
    for h in heads:
        (z0, r0), (z1, r1) = own[h], prev[h]
        back(i, h, z0, r0, jnp.zeros((1, ATT_TILE), F32), valid)
        off = r0[0:1, :]
        back(j_prev, h, z1, r1, off, has_prev)
        offs.append(off + jnp.where(has_prev, r1[0:1, :], 0.0))
    offs = tuple(offs)

    def least(offs):
        return functools.reduce(jnp.minimum, [jnp.min(off) for off in offs])

    def cond(carry):
        n, _, min_off = carry
        return jnp.logical_and(n < i, min_off <= SB_UNDERFLOW)

    def body(carry):
        n, offs, _ = carry
        j = i - 1 - n
        new = []
        for h in heads:
            z, r_in = front(j, h, False)
            back(j, h, z, r_in, offs[h], True)
            new.append(offs[h] + r_in[0:1, :])
        return n + 1, tuple(new), least(new)

    lax.while_loop(cond, body, (jnp.int32(1), offs, least(offs)))
    o_ref[...] = acc_ref[...].T.astype(BF16)


def _stick_breaking_attention(qk, vt):
    S = qk.shape[0]
    width = SB_HEADS_PER_STEP * LANES
    k_col0 = D_MODEL // width
    return pl.pallas_call(
        _sb_kernel,
        grid=(SB_HEADS // SB_HEADS_PER_STEP, S // ATT_TILE),
        in_specs=[pl.BlockSpec((ATT_TILE, width), lambda g, i: (i, g)),
                  pl.BlockSpec((S, width), lambda g, i: (0, k_col0 + g), pipeline_mode=pl.Buffered(1)),
                  pl.BlockSpec((S // ATT_TILE, width, ATT_TILE), lambda g, i: (0, g, 0),
                               pipeline_mode=pl.Buffered(1))],
        out_specs=pl.BlockSpec((ATT_TILE, width), lambda g, i: (i, g)),
        out_shape=jax.ShapeDtypeStruct((S, D_MODEL), BF16),
        scratch_shapes=[pltpu.VMEM((width, ATT_TILE), F32)],
        compiler_params=_params("parallel", "parallel"),
        name="stick_breaking",
    )(qk, qk, vt)


def _proj_ln_kernel(*refs, n_parts):
    o_refs = refs[:n_parts]
    w_refs = refs[n_parts:2 * n_parts]
    x_ref, g_ref, b_ref, out_ref = refs[2 * n_parts:]
    m = jnp.dot(o_refs[0][...], w_refs[0][...], preferred_element_type=F32)
    for o_ref, w_ref in zip(o_refs[1:], w_refs[1:]):
        m = m + jnp.dot(o_ref[...], w_ref[...], preferred_element_type=F32)
    out_ref[...] = _layer_norm(DEEPNORM_ALPHA * x_ref[...] + m, g_ref[...], b_ref[...])


def _proj_ln(o_parts, w_parts, x, g, b):
    S = x.shape[0]
    n = len(o_parts)
    in_specs = ([pl.BlockSpec((ROW_TILE, o.shape[1]), lambda i: (i, 0)) for o in o_parts]
                + [pl.BlockSpec(w.shape, lambda i: (0, 0)) for w in w_parts]
                + [pl.BlockSpec((ROW_TILE, D_MODEL), lambda i: (i, 0)),
                   pl.BlockSpec((1, D_MODEL), lambda i: (0, 0)),
                   pl.BlockSpec((1, D_MODEL), lambda i: (0, 0))])
    return pl.pallas_call(
        functools.partial(_proj_ln_kernel, n_parts=n),
        grid=(S // ROW_TILE,),
        in_specs=in_specs,
        out_specs=pl.BlockSpec((ROW_TILE, D_MODEL), lambda i: (i, 0)),
        out_shape=jax.ShapeDtypeStruct((S, D_MODEL), F32),
        compiler_params=_params("parallel"),
        name="out_proj_ln",
    )(*o_parts, *w_parts, x, g.reshape(1, D_MODEL), b.reshape(1, D_MODEL))


def _ffn_kernel(x_ref, wup_ref, cw_ref, cb_ref, wd_ref, g_ref, b_ref, out_ref,
                xb_ref, acc_ref, u_ref, h_ref, carry_ref):
    i = pl.program_id(0)
    rows = x_ref.shape[0]
    tf = u_ref.shape[-1]
    n_f = FFN_DIM // tf

    def cols(part, f):
        return slice(part * FFN_DIM + f * tf, part * FFN_DIM + (f + 1) * tf)

    @pl.when(i == 0)
    def _():
        carry_ref[...] = jnp.zeros_like(carry_ref)

    xb_ref[...] = x_ref[...].astype(BF16)

    def up(f):
        xb = xb_ref[...]
        return [jnp.dot(xb, wup_ref[:, cols(part, f)], preferred_element_type=F32) for part in range(2)]

    def load_u(f, slot, u):
        for part in range(2):
            u_ref[slot, part, 0:SUBLANES, :] = carry_ref[f, part]
            u_ref[slot, part, SUBLANES:SUBLANES + rows, :] = u[part]

    def down(f0, f1):
        part = jnp.dot(h_ref[:, f0 * tf:f1 * tf], wd_ref[f0 * tf:f1 * tf, :], preferred_element_type=F32)
        acc_ref[...] = part if f0 == 0 else acc_ref[...] + part

    def gate(f, slot):
        c = []
        for part in range(2):
            cw = cw_ref[:, cols(part, f)]
            acc = cb_ref[:, cols(part, f)]
            for tap in range(CONV_WIDTH):
                start = SUBLANES - (CONV_WIDTH - 1 - tap)
                acc = acc + cw[tap:tap + 1, :] * u_ref[slot, part, start:start + rows, :]
            c.append(acc)
            carry_ref[f, part] = u_ref[slot, part, rows:rows + SUBLANES, :]
        h_ref[:, f * tf:(f + 1) * tf] = (c[0] * (1.0 / (1.0 + jnp.exp(-c[0]))) * c[1]).astype(BF16)

    load_u(0, 0, up(0))
    done = 0
    for f in range(n_f):
        if f + 1 < n_f:
            u_next = up(f + 1)
        gate(f, f % 2)
        if f + 1 < n_f:
            load_u(f + 1, 1 - f % 2, u_next)
        if (f + 1) % FFN_DOWN_GROUP == 0 or f + 1 == n_f:
            down(done, f + 1)
            done = f + 1
    out_ref[...] = _layer_norm(DEEPNORM_ALPHA * x_ref[...] + acc_ref[...], g_ref[...], b_ref[...])


def _conv_ffn_ln(x, w_up, conv_w, conv_b, w_down, g, b):
    S = x.shape[0]
    tm, tf = FFN_ROW_TILE, FFN_COL_TILE
    n_f = FFN_DIM // tf
    whole = lambda a: pl.BlockSpec(a.shape, lambda i: (0,) * a.ndim, pipeline_mode=pl.Buffered(1))
    g2, b2, cb2 = g.reshape(1, D_MODEL), b.reshape(1, D_MODEL), conv_b.reshape(1, 2 * FFN_DIM)
    w_up_b, w_down_b = w_up.astype(BF16), w_down.astype(BF16)
    return pl.pallas_call(
        _ffn_kernel,
        grid=(S // tm,),
        in_specs=[pl.BlockSpec((tm, D_MODEL), lambda i: (i, 0)),
                  whole(w_up_b), whole(conv_w), whole(cb2), whole(w_down_b), whole(g2), whole(b2)],
        out_specs=pl.BlockSpec((tm, D_MODEL), lambda i: (i, 0)),
        out_shape=jax.ShapeDtypeStruct((S, D_MODEL), F32),
        scratch_shapes=[pltpu.VMEM((tm, D_MODEL), BF16),
                        pltpu.VMEM((tm, D_MODEL), F32),
                        pltpu.VMEM((2, 2, tm + 2 * SUBLANES, tf), F32),
                        pltpu.VMEM((tm, FFN_DIM), BF16),
                        pltpu.VMEM((n_f, 2, SUBLANES, tf), F32)],
        compiler_params=_params("arbitrary"),
        name="conv_ffn_ln",
    )(x, w_up_b, conv_w, cb2, w_down_b, g2, b2)


def kernel(x, w_qkv_ab, w_o_ab, w_qkv_c, w_o_c, ln_mix_g, ln_mix_b, w_up, conv_w, conv_b, w_down,
           ln_ffn_g, ln_ffn_b):
    batch, S, _ = x.shape
    assert S % ROW_TILE == 0 and S % FFN_ROW_TILE == 0 and S // MOBA_BLOCK <= HEAD_DIM
    rope_tables = _rope_tables(S)
    moba_width = MOBA_HEADS * HEAD_DIM
    outs = []
    for bi in range(batch):
        h = x[bi]
        for layer in range(DEPTH):
            if layer % 2 == 0:
                qk, vt = _qkv_proj(h, w_qkv_ab[layer // 2], HEAD_DIM ** -0.5 * LOG2_E, rope_tables)
                w_o = w_o_ab[layer // 2].astype(BF16)
                o_parts = [_moba_attention(qk, vt), _dilated_attention(qk, vt)]
                w_parts = [w_o[:moba_width], w_o[moba_width:]]
            else:
                qk, vt = _qkv_proj(h, w_qkv_c[layer // 2], SB_HEAD_DIM ** -0.5)
                o_parts = [_stick_breaking_attention(qk, vt)]
                w_parts = [w_o_c[layer // 2].astype(BF16)]
            h = _proj_ln(o_parts, w_parts, h, ln_mix_g[layer], ln_mix_b[layer])
            h = _conv_ffn_ln(h, w_up[layer], conv_w[layer], conv_b[layer], w_down[layer],
                             ln_ffn_g[layer], ln_ffn_b[layer])
        outs.append(h)
    return jnp.stack(outs)
```

```python
import functools

import numpy as np
import jax
import jax.numpy as jnp
from jax import lax
from jax.experimental import pallas as pl
from jax.experimental.pallas import tpu as pltpu

D_MODEL = 1024
DEPTH = 4
HEAD_DIM = 64
N_HEADS = D_MODEL // HEAD_DIM
MOBA_HEADS = N_HEADS // 4
SB_HEAD_DIM = 128
SB_HEADS = D_MODEL // SB_HEAD_DIM
MOBA_BLOCK = 256
MOBA_TOPK = 3
DILATED_PAIRS = ((128, 1), (512, 4), (2048, 16))
FFN_DIM = ((8 * D_MODEL // 3 + 127) // 128) * 128
CONV_WIDTH = 3
ROPE_THETA = 10000.0
LN_EPS = 1e-5
DEEPNORM_ALPHA = (2 * DEPTH) ** 0.25

LANES = 128
SUBLANES = 8
VMEM_LIMIT_BYTES = 56 * 1024 * 1024

ROW_TILE = 512
FFN_ROW_TILE = 512
FFN_COL_TILE = 256
FFN_DOWN_GROUP = 4
ATT_TILE = 256
ACC_ROWS = HEAD_DIM + 16
MAX_DILATED_WINDOW = max(w for w, _ in DILATED_PAIRS)
DILATED_KEY_TILES = MAX_DILATED_WINDOW // ATT_TILE + 1

NEG = -1e30
LOG2_E = float(np.log2(np.e))
SB_UNDERFLOW = 104.0
F32 = jnp.float32
BF16 = jnp.bfloat16

_CONTRACT_LANES = (((1,), (1,)), ((), ()))


def _params(*semantics, flags=None):
    return pltpu.CompilerParams(dimension_semantics=semantics, vmem_limit_bytes=VMEM_LIMIT_BYTES,
                                flags=flags)


def _layer_norm(y, g, b):
    mu = jnp.mean(y, axis=-1, keepdims=True)
    yc = y - mu
    var = jnp.mean(yc * yc, axis=-1, keepdims=True)
    return yc * lax.rsqrt(var + LN_EPS) * g + b


def _qkv_kernel(*refs, rope, q_scale):
    if rope:
        x_ref, w_ref, wvt_ref, cos_ref, sin_ref, qk_ref, vt_ref = refs
    else:
        x_ref, w_ref, wvt_ref, qk_ref, vt_ref = refs
    xb = x_ref[...].astype(BF16)
    for part in range(2):
        y = jnp.dot(xb, w_ref[:, part * D_MODEL:(part + 1) * D_MODEL], preferred_element_type=F32)
        if rope:
            c = cos_ref[...]
            s = sin_ref[...]
            lane = lax.broadcasted_iota(jnp.int32, c.shape, 1)
            first_half = (lane % HEAD_DIM) < (HEAD_DIM // 2)
            for cb in range(D_MODEL // LANES):
                yc = y[:, cb * LANES:(cb + 1) * LANES]
                partner = jnp.where(first_half,
                                    pltpu.roll(yc, LANES - HEAD_DIM // 2, 1),
                                    pltpu.roll(yc, HEAD_DIM // 2, 1))
                yc = yc * c + partner * s
                if part == 0:
                    yc = yc * q_scale
                qk_ref[:, part * D_MODEL + cb * LANES:part * D_MODEL + (cb + 1) * LANES] = yc.astype(BF16)
        else:
            if part == 0:
                y = y * q_scale
            qk_ref[:, part * D_MODEL:(part + 1) * D_MODEL] = y.astype(BF16)
    vt = lax.dot_general(wvt_ref[...], xb, _CONTRACT_LANES,
                         preferred_element_type=F32).astype(BF16)
    for t in range(ROW_TILE // ATT_TILE):
        vt_ref[t] = vt[:, t * ATT_TILE:(t + 1) * ATT_TILE]


def _qkv_proj(x, w_qkv, q_scale, rope_tables=None):
    w_qk = w_qkv[:, :2 * D_MODEL].astype(BF16)
    w_vt = w_qkv[:, 2 * D_MODEL:].T.astype(BF16)
    S = x.shape[0]
    rope = rope_tables is not None
    tiles = ROW_TILE // ATT_TILE
    in_specs = [pl.BlockSpec((ROW_TILE, D_MODEL), lambda i: (i, 0)),
                pl.BlockSpec((D_MODEL, 2 * D_MODEL), lambda i: (0, 0)),
                pl.BlockSpec((D_MODEL, D_MODEL), lambda i: (0, 0))]
    args = [x, w_qk, w_vt]
    if rope:
        in_specs += [pl.BlockSpec((ROW_TILE, LANES), lambda i: (i, 0))] * 2
        args += list(rope_tables)
    return pl.pallas_call(
        functools.partial(_qkv_kernel, rope=rope, q_scale=q_scale),
        grid=(S // ROW_TILE,),
        in_specs=in_specs,
        out_specs=[pl.BlockSpec((ROW_TILE, 2 * D_MODEL), lambda i: (i, 0)),
                   pl.BlockSpec((tiles, D_MODEL, ATT_TILE), lambda i: (i, 0, 0))],
        out_shape=[jax.ShapeDtypeStruct((S, 2 * D_MODEL), BF16),
                   jax.ShapeDtypeStruct((S // ATT_TILE, D_MODEL, ATT_TILE), BF16)],
        compiler_params=_params("parallel"),
        name="qkv_proj",
    )(*args)


def _rope_tables(S):
    inv = 1.0 / (ROPE_THETA ** (jnp.arange(0, HEAD_DIM, 2, dtype=F32) / HEAD_DIM))
    ang = jnp.arange(S, dtype=F32)[:, None] * inv[None, :]
    cos, sin = jnp.cos(ang), jnp.sin(ang)
    cos_t = jnp.tile(jnp.concatenate([cos, cos], axis=-1), (1, LANES // HEAD_DIM))
    sin_t = jnp.tile(jnp.concatenate([-sin, sin], axis=-1), (1, LANES // HEAD_DIM))
    return cos_t, sin_t


def _softmax_step(s_t, m_old):
    m_new = jnp.maximum(m_old, jnp.max(s_t, axis=0, keepdims=True))
    p_t = jnp.exp2(s_t - m_new)
    alpha = jnp.exp2(m_old - m_new)
    return p_t.astype(BF16), alpha, m_new


def _attention_sweep(first, last, scores, logits, values, acc_ref, s_ref, p_ref):
    acc_ref[...] = jnp.zeros_like(acc_ref)
    p_ref[...] = jnp.zeros_like(p_ref)
    stat0 = jnp.full((1, ATT_TILE), -jnp.inf, F32)
    one = jnp.ones((1, ATT_TILE), F32)
    ones_rows = jnp.ones((ACC_ROWS - HEAD_DIM, ATT_TILE), BF16)
    unrolled = isinstance(first, int) and isinstance(last, int)
    heads = range(s_ref.shape[1])

    def accumulate(t, slot, h, alpha):
        rows = slice(h * ACC_ROWS, (h + 1) * ACC_ROWS)
        lhs = jnp.concatenate([values(t, h), ones_rows], axis=0)
        acc_ref[rows, :] = alpha * acc_ref[rows, :] + jnp.dot(lhs, p_ref[slot, h],
                                                              preferred_element_type=F32)

    def softmax(t, slot, h, stat, is_last):
        p_t, alpha, m_new = _softmax_step(logits(t, h, s_ref[slot, h], is_last), stat)
        p_ref[slot, h] = p_t
        return alpha, m_new

    def trip(t, slot, carry):
        stats, alphas = carry
        prev = max(t - 1, first) if unrolled else jnp.maximum(t - 1, first)
        s_next = [scores(t + 1, h) for h in heads]
        new_stats, new_alphas = [], []
        for h in heads:
            accumulate(prev, 1 - slot, h, alphas[h])
            alpha, stat = softmax(t, slot, h, stats[h], False)
            s_ref[1 - slot, h] = s_next[h]
            new_stats.append(stat)
            new_alphas.append(alpha)
        return tuple(new_stats), tuple(new_alphas)

    def finish(slot, carry):
        stats, alphas = carry
        prev = max(last - 1, first) if unrolled else jnp.maximum(last - 1, first)
        inv = []
        for h in heads:
            accumulate(prev, 1 - slot, h, alphas[h])
            alpha, _ = softmax(last, slot, h, stats[h], True)
            accumulate(last, slot, h, alpha)
            inv.append(1.0 / acc_ref[h * ACC_ROWS + HEAD_DIM:h * ACC_ROWS + HEAD_DIM + 1, :])
        return tuple(inv)

    for h in heads:
        s_ref[0, h] = scores(first, h)
    carry = (tuple(stat0 for _ in heads), tuple(one for _ in heads))
    if unrolled:
        for t in range(first, last):
            carry = trip(t, (t - first) % 2, carry)
        inv = finish((last - first) % 2, carry)
    else:
        n = last - first

        def quad(k, carry):
            t = first + 4 * k
            for d in range(4):
                carry = trip(t + d, d % 2, carry)
            return carry

        def pair(k, carry):
            t = first + (n // 4) * 4 + 2 * k
            return trip(t + 1, 1, trip(t, 0, carry))

        carry = lax.fori_loop(0, n // 4, quad, carry)
        carry = lax.fori_loop(0, (n % 4) // 2, pair, carry)
        inv = lax.cond(n % 2 == 1,
                       lambda c: finish(1, trip(last - 1, 0, c)),
                       lambda c: finish(0, c), carry)
    return jnp.concatenate([acc_ref[h * ACC_ROWS:h * ACC_ROWS + HEAD_DIM, :] * inv[h] for h in heads], axis=0)


def _moba_select_kernel(q_ref, k_ref, qa_ref, ka_ref, km_ref):
    i = pl.program_id(1)
    n_blocks = k_ref.shape[0] // MOBA_BLOCK

    @pl.when(i == 0)
    def _():
        km_ref[...] = jnp.zeros_like(km_ref)

        def body(b, carry):
            kb = k_ref[pl.ds(pl.multiple_of(b * MOBA_BLOCK, MOBA_BLOCK), MOBA_BLOCK), :].astype(F32)
            km_ref[pl.ds(b, 1), :] = jnp.sum(kb, axis=0, keepdims=True) * (1.0 / MOBA_BLOCK)
            return carry

        lax.fori_loop(0, n_blocks, body, 0)

    q = q_ref[...].astype(F32)
    kt = k_ref[pl.ds(pl.multiple_of(i * ATT_TILE, ATT_TILE), ATT_TILE), :].astype(F32)
    lane_i = lax.broadcasted_iota(jnp.int32, q.shape, 1)
    lane = lane_i.astype(F32)
    low = lane_i < HEAD_DIM
    blk = i.astype(F32)
    km = km_ref[...]
    onehot = jnp.where(lane_i - HEAD_DIM == i, 1.0, 0.0)
    q_sw = pltpu.roll(q, HEAD_DIM, 1)
    k_sw = pltpu.roll(kt, HEAD_DIM, 1)
    row = lax.broadcasted_iota(jnp.int32, (LANES, ATT_TILE), 0).astype(F32)
    for h in range(2):
        qh = jnp.where(low if h == 0 else jnp.logical_not(low), q, 0.0)
        gate = lax.dot_general(km, qh, _CONTRACT_LANES, precision=lax.Precision.HIGHEST,
                               preferred_element_type=F32)
        g = jnp.where(row < blk, gate, -jnp.inf)
        bias_t = jnp.where(row == blk, 0.0, NEG)
        for _ in range(MOBA_TOPK):
            m = jnp.max(g, axis=0, keepdims=True)
            first = jnp.min(jnp.where(g == m, row, float(LANES)), axis=0, keepdims=True)
            first = jnp.where(m > -jnp.inf, first, -1.0)
            pick = row == first
            bias_t = jnp.where(pick, 0.0, bias_t)
            g = jnp.where(pick, -jnp.inf, g)
        bias = pltpu.roll(bias_t.T, HEAD_DIM, 1)
        qa_ref[h] = jnp.where(low, q if h == 0 else q_sw, bias).astype(BF16)
        ka_ref[h] = jnp.where(low, kt if h == 0 else k_sw, onehot).astype(BF16)


def _moba_flash_kernel(qa_ref, ka_ref, vt_ref, o_ref, acc_ref, s_ref, p_ref):
    i = pl.program_id(1)
    key = lax.broadcasted_iota(jnp.int32, (ATT_TILE, ATT_TILE), 0)
    qry = lax.broadcasted_iota(jnp.int32, (ATT_TILE, ATT_TILE), 1)

    def scores(j, h):
        off = pl.multiple_of(j * ATT_TILE, ATT_TILE)
        return lax.dot_general(ka_ref[h, pl.ds(off, ATT_TILE), :], qa_ref[h], _CONTRACT_LANES,
                               preferred_element_type=F32)

    def logits(j, h, s_t, is_last):
        return jnp.where(key <= qry, s_t, -jnp.inf) if is_last else s_t

    def values(j, h):
        return vt_ref[j, h * HEAD_DIM:(h + 1) * HEAD_DIM, :]

    o_t = _attention_sweep(0, i, scores, logits, values, acc_ref, s_ref, p_ref)
    o_ref[...] = o_t.T.astype(BF16)


def _moba_attention(qk, vt):
    S = qk.shape[0]
    n_pairs = MOBA_HEADS // 2
    k_col0 = D_MODEL // LANES
    aug = jax.ShapeDtypeStruct((MOBA_HEADS, S, LANES), BF16)
    qa, ka = pl.pallas_call(
        _moba_select_kernel,
        grid=(n_pairs, S // ATT_TILE),
        in_specs=[pl.BlockSpec((ATT_TILE, LANES), lambda p, i: (i, p)),
                  pl.BlockSpec((S, LANES), lambda p, i: (0, k_col0 + p))],
        out_specs=[pl.BlockSpec((2, ATT_TILE, LANES), lambda p, i: (p, i, 0))] * 2,
        out_shape=[aug, aug],
        scratch_shapes=[pltpu.VMEM((LANES, LANES), F32)],
        compiler_params=_params("parallel", "arbitrary"),
        name="moba_select",
    )(qk, qk)
    return pl.pallas_call(
        _moba_flash_kernel,
        grid=(n_pairs, S // ATT_TILE),
        in_specs=[pl.BlockSpec((2, ATT_TILE, LANES), lambda p, i: (p, i, 0)),
                  pl.BlockSpec((2, S, LANES), lambda p, i: (p, 0, 0)),
                  pl.BlockSpec((S // ATT_TILE, LANES, ATT_TILE), lambda p, i: (0, p, 0))],
        out_specs=pl.BlockSpec((ATT_TILE, LANES), lambda p, i: (i, p)),
        out_shape=jax.ShapeDtypeStruct((S, n_pairs * LANES), BF16),
        scratch_shapes=[pltpu.VMEM((2 * ACC_ROWS, ATT_TILE), F32),
                        pltpu.VMEM((2, 2, ATT_TILE, ATT_TILE), F32),
                        pltpu.VMEM((2, 2, ATT_TILE, ATT_TILE), BF16)],
        compiler_params=_params("parallel", "parallel"),
        name="moba_flash",
    )(qa, ka, vt)


def _dilated_bias_table():
    key = np.arange(ATT_TILE)[:, None]
    qry = np.arange(ATT_TILE)[None, :]
    tiles = []
    for t in range(DILATED_KEY_TILES):
        d = qry - key + ATT_TILE * (DILATED_KEY_TILES - 1 - t)
        count = np.zeros_like(d)
        for window, dil in DILATED_PAIRS:
            count += ((d >= 0) & (d <= window) & (d % dil == 0)).astype(d.dtype)
        tiles.append(np.where(count > 0, np.log2(np.maximum(count, 1)), NEG))
    tiles.append(np.full_like(tiles[0], NEG))
    return jnp.asarray(np.stack(tiles), dtype=F32)


DILATED_PAIRS_PER_STEP = 2


def _dilated_kernel(q_ref, k_ref, vt_ref, tab_ref, o_ref, acc_ref, s_ref, p_ref, qh_ref):
    i = pl.program_id(1)
    last = DILATED_KEY_TILES - 1

    def tile(t):
        return jnp.maximum(i - last + t, 0)

    def logits(t, h, s_t, is_last):
        return s_t + tab_ref[jnp.where(i - last + t < 0, DILATED_KEY_TILES, t)]

    for pair in range(DILATED_PAIRS_PER_STEP):
        lanes = slice(pair * LANES, (pair + 1) * LANES)
        q = q_ref[:, lanes]
        low = lax.broadcasted_iota(jnp.int32, q.shape, 1) < HEAD_DIM
        zero = jnp.zeros_like(q)
        qh_ref[pair, 0] = jnp.where(low, q, zero)
        qh_ref[pair, 1] = jnp.where(low, zero, q)

        def scores(t, h, pair=pair, lanes=lanes):
            off = pl.multiple_of(tile(t) * ATT_TILE, ATT_TILE)
            return lax.dot_general(k_ref[pl.ds(off, ATT_TILE), lanes], qh_ref[pair, h], _CONTRACT_LANES,
                                   preferred_element_type=F32)

        def values(t, h, pair=pair):
            row0 = pair * LANES + h * HEAD_DIM
            return vt_ref[tile(t), row0:row0 + HEAD_DIM, :]

        o_t = _attention_sweep(0, last, scores, logits, values, acc_ref.at[pair], s_ref.at[pair], p_ref.at[pair])
        o_ref[:, lanes] = o_t.T.astype(BF16)


def _dilated_attention(qk, vt):
    S = qk.shape[0]
    width = DILATED_PAIRS_PER_STEP * LANES
    n_groups = (N_HEADS - MOBA_HEADS) * HEAD_DIM // width
    q_col0 = MOBA_HEADS * HEAD_DIM // width
    k_col0 = D_MODEL // width + q_col0
    per = DILATED_PAIRS_PER_STEP
    return pl.pallas_call(
        _dilated_kernel,
        grid=(n_groups, S // ATT_TILE),
        in_specs=[pl.BlockSpec((ATT_TILE, width), lambda g, i: (i, q_col0 + g)),
                  pl.BlockSpec((S, width), lambda g, i: (0, k_col0 + g)),
                  pl.BlockSpec((S // ATT_TILE, width, ATT_TILE), lambda g, i: (0, q_col0 + g, 0)),
                  pl.BlockSpec((DILATED_KEY_TILES + 1, ATT_TILE, ATT_TILE), lambda g, i: (0, 0, 0))],
        out_specs=pl.BlockSpec((ATT_TILE, width), lambda g, i: (i, g)),
        out_shape=jax.ShapeDtypeStruct((S, n_groups * width), BF16),
        scratch_shapes=[pltpu.VMEM((per, 2 * ACC_ROWS, ATT_TILE), F32),
                        pltpu.VMEM((per, 2, 2, ATT_TILE, ATT_TILE), F32),
                        pltpu.VMEM((per, 2, 2, ATT_TILE, ATT_TILE), BF16),
                        pltpu.VMEM((per, 2, ATT_TILE, LANES), BF16)],
        compiler_params=_params("parallel", "parallel"),
        name="dilated_attention",
    )(qk, qk, vt, _dilated_bias_table())


SB_HEADS_PER_STEP = 4


def _sb_kernel(q_ref, k_ref, vt_ref, o_ref, acc_ref):
    i = pl.program_id(1)
    heads = range(SB_HEADS_PER_STEP)
    key = lax.broadcasted_iota(jnp.int32, (ATT_TILE, ATT_TILE), 0)
    qry = lax.broadcasted_iota(jnp.int32, (ATT_TILE, ATT_TILE), 1)
    suffix = jnp.where(qry >= key, 1.0, 0.0).astype(BF16)
    acc_ref[...] = jnp.zeros_like(acc_ref)
    valid = key < qry

    def lanes(h):
        return slice(h * LANES, (h + 1) * LANES)

    def front(j, h, diagonal):
        k = k_ref[pl.ds(pl.multiple_of(j * ATT_TILE, ATT_TILE), ATT_TILE), lanes(h)]
        z = lax.dot_general(k, q_ref[:, lanes(h)], _CONTRACT_LANES, preferred_element_type=F32)
        sp = jnp.maximum(z, 0.0) + jnp.log(1.0 + jnp.exp(-jnp.abs(z)))
        if diagonal:
            sp = jnp.where(valid, sp, 0.0)
        sp_hi = sp.astype(BF16)
        sp_lo = (sp - sp_hi.astype(F32)).astype(BF16)
        r_in = (jnp.dot(suffix, sp_hi, preferred_element_type=F32)
                + jnp.dot(suffix, sp_lo, preferred_element_type=F32))
        return z, r_in

    def back(j, h, z, r_in, off, keep):
        a = jnp.where(keep, jnp.exp(z - (r_in + off)), 0.0)
        acc_ref[lanes(h), :] += jnp.dot(vt_ref[j, lanes(h), :], a.astype(BF16), preferred_element_type=F32)

    has_prev = i > 0
    j_prev = jnp.maximum(i - 1, 0)
    own = [front(i, h, True) for h in heads]
    prev = [front(j_prev, h, False) for h in heads]
    offs = []
    for h in heads:
        (z0, r0), (z1, r1) = own[h], prev[h]
        back(i, h, z0, r0, jnp.zeros((1, ATT_TILE), F32), valid)
        off = r0[0:1, :]
        back(j_prev, h, z1, r1, off, has_prev)
        offs.append(off + jnp.where(has_prev, r1[0:1, :], 0.0))
    offs = tuple(offs)

    def least(offs):
        return functools.reduce(jnp.minimum, [jnp.min(off) for off in offs])

    def cond(carry):
        n, _, min_off = carry
        return jnp.logical_and(n < i, min_off <= SB_UNDERFLOW)

    def body(carry):
        n, offs, _ = carry
        j = i - 1 - n
        new = []
        for h in heads:
            z, r_in = front(j, h, False)
            back(j, h, z, r_in, offs[h], True)
            new.append(offs[h] + r_in[0:1, :])
        return n + 1, tuple(new), least(new)

    lax.while_loop(cond, body, (jnp.int32(1), offs, least(offs)))
    o_ref[...] = acc_ref[...].T.astype(BF16)


def _stick_breaking_attention(qk, vt):
    S = qk.shape[0]
    width = SB_HEADS_PER_STEP * LANES
    k_col0 = D_MODEL // width
    return pl.pallas_call(
        _sb_kernel,
        grid=(SB_HEADS // SB_HEADS_PER_STEP, S // ATT_TILE),
        in_specs=[pl.BlockSpec((ATT_TILE, width), lambda g, i: (i, g)),
                  pl.BlockSpec((S, width), lambda g, i: (0, k_col0 + g), pipeline_mode=pl.Buffered(1)),
                  pl.BlockSpec((S // ATT_TILE, width, ATT_TILE), lambda g, i: (0, g, 0),
                               pipeline_mode=pl.Buffered(1))],
        out_specs=pl.BlockSpec((ATT_TILE, width), lambda g, i: (i, g)),
        out_shape=jax.ShapeDtypeStruct((S, D_MODEL), BF16),
        scratch_shapes=[pltpu.VMEM((width, ATT_TILE), F32)],
        compiler_params=_params("parallel", "parallel"),
        name="stick_breaking",
    )(qk, qk, vt)


def _proj_ln_kernel(*refs, n_parts):
    o_refs = refs[:n_parts]
    w_refs = refs[n_parts:2 * n_parts]
    x_ref, g_ref, b_ref, out_ref = refs[2 * n_parts:]
    m = jnp.dot(o_refs[0][...], w_refs[0][...], preferred_element_type=F32)
    for o_ref, w_ref in zip(o_refs[1:], w_refs[1:]):
        m = m + jnp.dot(o_ref[...], w_ref[...], preferred_element_type=F32)
    out_ref[...] = _layer_norm(DEEPNORM_ALPHA * x_ref[...] + m, g_ref[...], b_ref[...])


def _proj_ln(o_parts, w_parts, x, g, b):
    S = x.shape[0]
    n = len(o_parts)
    in_specs = ([pl.BlockSpec((ROW_TILE, o.shape[1]), lambda i: (i, 0)) for o in o_parts]
                + [pl.BlockSpec(w.shape, lambda i: (0, 0)) for w in w_parts]
                + [pl.BlockSpec((ROW_TILE, D_MODEL), lambda i: (i, 0)),
                   pl.BlockSpec((1, D_MODEL), lambda i: (0, 0)),
                   pl.BlockSpec((1, D_MODEL), lambda i: (0, 0))])
    return pl.pallas_call(
        functools.partial(_proj_ln_kernel, n_parts=n),
        grid=(S // ROW_TILE,),
        in_specs=in_specs,
        out_specs=pl.BlockSpec((ROW_TILE, D_MODEL), lambda i: (i, 0)),
        out_shape=jax.ShapeDtypeStruct((S, D_MODEL), F32),
        compiler_params=_params("parallel"),
        name="out_proj_ln",
    )(*o_parts, *w_parts, x, g.reshape(1, D_MODEL), b.reshape(1, D_MODEL))


def _ffn_kernel(x_ref, wup_ref, cw_ref, cb_ref, wd_ref, g_ref, b_ref, out_ref,
                xb_ref, acc_ref, u_ref, h_ref, carry_ref):
    i = pl.program_id(0)
    rows = x_ref.shape[0]
    tf = u_ref.shape[-1]
    n_f = FFN_DIM // tf

    def cols(part, f):
        return slice(part * FFN_DIM + f * tf, part * FFN_DIM + (f + 1) * tf)

    @pl.when(i == 0)
    def _():
        carry_ref[...] = jnp.zeros_like(carry_ref)

    xb_ref[...] = x_ref[...].astype(BF16)

    def up(f):
        xb = xb_ref[...]
        return [jnp.dot(xb, wup_ref[:, cols(part, f)], preferred_element_type=F32) for part in range(2)]

    def load_u(f, slot, u):
        for part in range(2):
            u_ref[slot, part, 0:SUBLANES, :] = carry_ref[f, part]
            u_ref[slot, part, SUBLANES:SUBLANES + rows, :] = u[part]

    def down(f0, f1):
        part = jnp.dot(h_ref[:, f0 * tf:f1 * tf], wd_ref[f0 * tf:f1 * tf, :], preferred_element_type=F32)
        acc_ref[...] = part if f0 == 0 else acc_ref[...] + part

    def gate(f, slot):
        c = []
        for part in range(2):
            cw = cw_ref[:, cols(part, f)]
            acc = cb_ref[:, cols(part, f)]
            for tap in range(CONV_WIDTH):
                start = SUBLANES - (CONV_WIDTH - 1 - tap)
                acc = acc + cw[tap:tap + 1, :] * u_ref[slot, part, start:start + rows, :]
            c.append(acc)
            carry_ref[f, part] = u_ref[slot, part, rows:rows + SUBLANES, :]
        h_ref[:, f * tf:(f + 1) * tf] = (c[0] * (1.0 / (1.0 + jnp.exp(-c[0]))) * c[1]).astype(BF16)

    load_u(0, 0, up(0))
    done = 0
    for f in range(n_f):
        if f + 1 < n_f:
            u_next = up(f + 1)
        gate(f, f % 2)
        if f + 1 < n_f:
            load_u(f + 1, 1 - f % 2, u_next)
        if (f + 1) % FFN_DOWN_GROUP == 0 or f + 1 == n_f:
            down(done, f + 1)
            done = f + 1
    out_ref[...] = _layer_norm(DEEPNORM_ALPHA * x_ref[...] + acc_ref[...], g_ref[...], b_ref[...])


def _conv_ffn_ln(x, w_up, conv_w, conv_b, w_down, g, b):
    S = x.shape[0]
    tm, tf = FFN_ROW_TILE, FFN_COL_TILE
    n_f = FFN_DIM // tf
    whole = lambda a: pl.BlockSpec(a.shape, lambda i: (0,) * a.ndim, pipeline_mode=pl.Buffered(1))
    g2, b2, cb2 = g.reshape(1, D_MODEL), b.reshape(1, D_MODEL), conv_b.reshape(1, 2 * FFN_DIM)
    w_up_b, w_down_b = w_up.astype(BF16), w_down.astype(BF16)
    return pl.pallas_call(
        _ffn_kernel,
        grid=(S // tm,),
        in_specs=[pl.BlockSpec((tm, D_MODEL), lambda i: (i, 0)),
                  whole(w_up_b), whole(conv_w), whole(cb2), whole(w_down_b), whole(g2), whole(b2)],
        out_specs=pl.BlockSpec((tm, D_MODEL), lambda i: (i, 0)),
        out_shape=jax.ShapeDtypeStruct((S, D_MODEL), F32),
        scratch_shapes=[pltpu.VMEM((tm, D_MODEL), BF16),
                        pltpu.VMEM((tm, D_MODEL), F32),
                        pltpu.VMEM((2, 2, tm + 2 * SUBLANES, tf), F32),
                        pltpu.VMEM((tm, FFN_DIM), BF16),
                        pltpu.VMEM((n_f, 2, SUBLANES, tf), F32)],
        compiler_params=_params("arbitrary"),
        name="conv_ffn_ln",
    )(x, w_up_b, conv_w, cb2, w_down_b, g2, b2)


def kernel(x, w_qkv_ab, w_o_ab, w_qkv_c, w_o_c, ln_mix_g, ln_mix_b, w_up, conv_w, conv_b, w_down,
           ln_ffn_g, ln_ffn_b):
    batch, S, _ = x.shape
    assert S % ROW_TILE == 0 and S % FFN_ROW_TILE == 0 and S // MOBA_BLOCK <= HEAD_DIM
    rope_tables = _rope_tables(S)
    moba_width = MOBA_HEADS * HEAD_DIM
    outs = []
    for bi in range(batch):
        h = x[bi]
        for layer in range(DEPTH):
            if layer % 2 == 0:
                qk, vt = _qkv_proj(h, w_qkv_ab[layer // 2], HEAD_DIM ** -0.5 * LOG2_E, rope_tables)
                w_o = w_o_ab[layer // 2].astype(BF16)
                o_parts = [_moba_attention(qk, vt), _dilated_attention(qk, vt)]
                w_parts = [w_o[:moba_width], w_o[moba_width:]]
            else:
                qk, vt = _qkv_proj(h, w_qkv_c[layer // 2], SB_HEAD_DIM ** -0.5)
                o_parts = [_stick_breaking_attention(qk, vt)]
                w_parts = [w_o_c[layer // 2].astype(BF16)]
            h = _proj_ln(o_parts, w_parts, h, ln_mix_g[layer], ln_mix_b[layer])
            h = _conv_ffn_ln(h, w_up[layer], conv_w[layer], conv_b[layer], w_down[layer],
                             ln_ffn_g[layer], ln_ffn_b[layer])
        outs.append(h)
    return jnp.stack(outs)
```
